```python
import jax, jax.numpy as jnp
from jax import lax
import numpy as np

D_MODEL = 2048
BATCH = 2
SEQ = 16384
DEPTH = 1
DEC_BATCH = 4
DEC_SEQ = 8192
PAST_LEN = 128

N_HEADS_A = 8
DK_A = 128
DV_A = 256
W_QK = N_HEADS_A * DK_A
W_V = N_HEADS_A * DV_A
N_GATES = 4 * N_HEADS_A
CONV_W = 5
CHUNK = 128
N_GROUPS_B = 8
GROUP_B = 128
W_B = N_GROUPS_B * GROUP_B
D_FF = ((8 * D_MODEL // 3 + 255) // 256) * 256
OFF_Q = 0
OFF_K = OFF_Q + W_QK
OFF_V = OFF_K + W_QK
OFF_O = OFF_V + W_V
OFF_G = OFF_O + W_V
OFF_B = OFF_G + N_GATES
OFF_M = OFF_B + W_B
W_IN = OFF_M + 2 * D_MODEL
EPS = 1e-6

kernel_name = "hybrid_mlstm_fnet_gated_encoder"


def rmsnorm(x, g):
    xf = x.astype(jnp.float32)
    y = xf * lax.rsqrt(jnp.mean(xf * xf, axis=-1, keepdims=True) + EPS) * g.astype(jnp.float32)
    return y.astype(x.dtype)


def short_conv(u, w):
    c = u.shape[-1]
    return lax.conv_general_dilated(
        u, w[:, None, :].astype(u.dtype), window_strides=(1,),
        padding=[(CONV_W // 2, CONV_W // 2)],
        dimension_numbers=("NWC", "WIO", "NWC"), feature_group_count=c)


def mlstm_chunkwise(q, k, v, log_i, log_f):
    B, T, H, _ = q.shape
    nc = T // CHUNK

    def chunks(a):
        return a.reshape((B, nc, CHUNK) + a.shape[2:]).transpose(1, 0, 3, 2, 4)

    def gchunks(a):
        return a.reshape(B, nc, CHUNK, H).transpose(1, 0, 3, 2)

    tril = jnp.tril(jnp.ones((CHUNK, CHUNK), dtype=bool))

    def step(carry, inp):
        C, n, m = carry
        qc, kc, vc, li, lf = inp
        b = jnp.cumsum(lf, axis=-1)
        dmat = b[..., :, None] - b[..., None, :] + li[..., None, :]
        dmat = jnp.where(tril, dmat, -jnp.inf)
        inter = b + m[..., None]
        m_t = jnp.maximum(inter, jnp.max(dmat, axis=-1))
        w_inter = jnp.exp(inter - m_t)
        s = jnp.einsum("bhtd,bhsd->bhts", qc, kc) * jnp.exp(dmat - m_t[..., None])
        num = (w_inter[..., None] * jnp.einsum("bhvd,bhtd->bhtv", C, qc)
               + jnp.einsum("bhts,bhsv->bhtv", s, vc))
        nq = w_inter * jnp.einsum("bhd,bhtd->bht", n, qc) + jnp.sum(s, axis=-1)
        h = num / jnp.maximum(jnp.abs(nq), jnp.exp(-m_t))[..., None]
        b_last = b[..., -1]
        dec = b_last[..., None] - b + li
        m_new = jnp.maximum(b_last + m, jnp.max(dec, axis=-1))
        w_c = jnp.exp(b_last + m - m_new)
        w_s = jnp.exp(dec - m_new[..., None])
        C_new = w_c[..., None, None] * C + jnp.einsum("bhs,bhsv,bhsd->bhvd", w_s, vc, kc)
        n_new = w_c[..., None] * n + jnp.einsum("bhs,bhsd->bhd", w_s, kc)
        return (C_new, n_new, m_new), h

    init = (jnp.zeros((B, H, DV_A, DK_A), jnp.float32),
            jnp.zeros((B, H, DK_A), jnp.float32),
            jnp.zeros((B, H), jnp.float32))
    _, hs = lax.scan(step, init, (chunks(q), chunks(k), chunks(v), gchunks(log_i), gchunks(log_f)))
    return hs.transpose(1, 0, 3, 2, 4).reshape(B, T, H, DV_A)


def encoder_layer(x, g_pre_mix, w_in, conv_w, b_gates, g_head, w_a_out, w_b_out,
                  b_merge, w_out, g_post_mix, g_pre_ffn, w_ffn_in, w_ffn_out, g_post_ffn):
    B, T, _ = x.shape
    dt = x.dtype
    h = rmsnorm(x, g_pre_mix)
    z = h @ w_in
    qk_raw = z[..., OFF_Q:OFF_V]
    v_raw = z[..., OFF_V:OFF_O]
    o_pre = z[..., OFF_O:OFF_G]
    gate_pre = z[..., OFF_G:OFF_B]
    u_b = z[..., OFF_B:OFF_M]
    merge_pre = z[..., OFF_M:]

    qk = jax.nn.silu(short_conv(qk_raw, conv_w)).astype(jnp.float32)
    q = qk[..., :W_QK].reshape(B, T, N_HEADS_A, DK_A)
    k = qk[..., W_QK:].reshape(B, T, N_HEADS_A, DK_A) * (DK_A ** -0.5)
    v = v_raw.astype(jnp.float32).reshape(B, T, N_HEADS_A, DV_A)
    g = gate_pre.astype(jnp.float32) + b_gates.astype(jnp.float32)
    li_f, f_f, li_b, f_b = jnp.split(g, 4, axis=-1)
    lf_f = jax.nn.log_sigmoid(f_f)
    lf_b = jax.nn.log_sigmoid(f_b)
    h_fwd = mlstm_chunkwise(q, k, v, li_f, lf_f)
    h_bwd = jnp.flip(mlstm_chunkwise(jnp.flip(q, 1), jnp.flip(k, 1), jnp.flip(v, 1),
                                     jnp.flip(li_b, 1), jnp.flip(lf_b, 1)), axis=1)
    ha = h_fwd + h_bwd
    ha = ha * lax.rsqrt(jnp.mean(ha * ha, axis=-1, keepdims=True) + EPS)
    ha = (ha.reshape(B, T, W_V) * g_head.astype(jnp.float32)).astype(dt) * jax.nn.sigmoid(o_pre)
    y_a = ha @ w_a_out

    ub = u_b.astype(jnp.float32).reshape(B, T, N_GROUPS_B, GROUP_B)
    fb = jnp.fft.fft2(ub, axes=(1, 3), norm="ortho").real
    y_b = fb.astype(dt).reshape(B, T, W_B) @ w_b_out

    gates = jax.nn.sigmoid(merge_pre + b_merge)
    g_a = gates[..., :D_MODEL]
    g_b = gates[..., D_MODEL:]
    mix = (g_a * y_a + g_b * y_b) @ w_out
    x = x + rmsnorm(mix, g_post_mix)

    h2 = rmsnorm(x, g_pre_ffn)
    gu = h2 @ w_ffn_in
    ff = (jax.nn.silu(gu[..., :D_FF]) * gu[..., D_FF:]) @ w_ffn_out
    return x + rmsnorm(ff, g_post_ffn)


def run_trunk(x, g_pre_mix, w_in, conv_w, b_gates, g_head, w_a_out, w_b_out,
              b_merge, w_out, g_post_mix, g_pre_ffn, w_ffn_in, w_ffn_out, g_post_ffn):
    for l in range(DEPTH):
        x = encoder_layer(x, g_pre_mix[l], w_in[l], conv_w[l], b_gates[l], g_head[l],
                          w_a_out[l], w_b_out[l], b_merge[l], w_out[l], g_post_mix[l],
                          g_pre_ffn[l], w_ffn_in[l], w_ffn_out[l], g_post_ffn[l])
    return x


def setup_inputs(seed: int = 0) -> dict:
    key = jax.random.key(seed)
    ks = jax.random.split(key, 20)
    f32 = jnp.float32

    def nrm(k, shape, fan_in):
        return jax.random.normal(k, shape, f32) * (fan_in ** -0.5)

    def gain(k, shape):
        return 1.0 + 0.02 * jax.random.normal(k, shape, f32)

    i_bias = 0.1 * jax.random.normal(ks[14], (DEPTH, 2, N_HEADS_A), f32)
    f_bias = (jnp.linspace(3.0, 6.0, N_HEADS_A, dtype=f32)[None, None, :]
              + 0.1 * jax.random.normal(ks[15], (DEPTH, 2, N_HEADS_A), f32))
    b_gates = jnp.stack([i_bias, f_bias], axis=2).reshape(DEPTH, N_GATES)

    return {
        "x_prompt": jax.random.normal(ks[0], (BATCH, SEQ, D_MODEL), f32),
        "x_sample": jax.random.normal(ks[1], (DEC_BATCH, DEC_SEQ, D_MODEL), f32),
        "g_pre_mix": gain(ks[2], (DEPTH, D_MODEL)),
        "w_in": nrm(ks[3], (DEPTH, D_MODEL, W_IN), D_MODEL),
        "conv_w": nrm(ks[4], (DEPTH, CONV_W, 2 * W_QK), CONV_W),
        "b_gates": b_gates,
        "g_head": gain(ks[5], (DEPTH, W_V)),
        "w_a_out": nrm(ks[6], (DEPTH, W_V, D_MODEL), W_V),
        "w_b_out": nrm(ks[7], (DEPTH, W_B, D_MODEL), W_B),
        "b_merge": 0.02 * jax.random.normal(ks[8], (DEPTH, 2 * D_MODEL), f32),
        "w_out": nrm(ks[9], (DEPTH, D_MODEL, D_MODEL), D_MODEL),
        "g_post_mix": gain(ks[10], (DEPTH, D_MODEL)),
        "g_pre_ffn": gain(ks[11], (DEPTH, D_MODEL)),
        "w_ffn_in": nrm(ks[12], (DEPTH, D_MODEL, 2 * D_FF), D_MODEL),
        "w_ffn_out": nrm(ks[13], (DEPTH, D_FF, D_MODEL), D_FF),
        "g_post_ffn": gain(ks[16], (DEPTH, D_MODEL)),
    }


def reference(x_prompt, x_sample, g_pre_mix, w_in, conv_w, b_gates, g_head, w_a_out, w_b_out,
              b_merge, w_out, g_post_mix, g_pre_ffn, w_ffn_in, w_ffn_out, g_post_ffn):
    y_prompt = run_trunk(x_prompt, g_pre_mix, w_in, conv_w, b_gates, g_head, w_a_out, w_b_out,
                         b_merge, w_out, g_post_mix, g_pre_ffn, w_ffn_in, w_ffn_out, g_post_ffn)
    y_sample = run_trunk(x_sample, g_pre_mix, w_in, conv_w, b_gates, g_head, w_a_out, w_b_out,
                         b_merge, w_out, g_post_mix, g_pre_ffn, w_ffn_in, w_ffn_out, g_post_ffn)
    return (y_prompt, y_sample)
```

```python
import functools

import numpy as np
import jax
import jax.numpy as jnp
from jax import lax
from jax.experimental import pallas as pl
from jax.experimental.pallas import tpu as pltpu

F32 = jnp.float32
BF16 = jnp.bfloat16

D_MODEL = 2048
N_HEADS = 8
DK = 128
DV = 256
W_QK = N_HEADS * DK
W_V = N_HEADS * DV
N_GATES = 4 * N_HEADS
CONV_W = 5
CHUNK = 128
N_GROUPS_B = 8
GROUP_B = 128
W_B = N_GROUPS_B * GROUP_B
D_FF = 5632
OFF_V = 2 * W_QK
OFF_O = OFF_V + W_V
OFF_G = OFF_O + W_V
OFF_B = OFF_G + N_GATES
OFF_M = OFF_B + W_B
EPS = 1e-6

LANES = 128
SUBLANES = 8
MIB = 1024 * 1024


def _cparams(sem, vmem_mib):
    return pltpu.CompilerParams(dimension_semantics=sem, vmem_limit_bytes=vmem_mib * MIB)


IN_TM = 512
IN_TN = 1024
IN_NJ = (2 * W_QK + 2 * W_V + 2 * D_MODEL + W_B) // IN_TN


def _inproj_body(x_ref, g_ref, w_ref, wg_ref,
                 zq_ref, zv_ref, zo_ref, zm_ref, zu_ref, zg_ref, h_scr):
    j = pl.program_id(1)

    @pl.when(j == 0)
    def _():
        x = x_ref[...]
        ms = jnp.mean(x * x, axis=-1, keepdims=True)
        h = (x * lax.rsqrt(ms + EPS) * g_ref[...]).astype(BF16)
        h_scr[...] = h
        zg_ref[...] = jnp.dot(h, wg_ref[...], preferred_element_type=F32)

    acc = jnp.dot(h_scr[...], w_ref[...], preferred_element_type=F32)

    @pl.when(j < 2)
    def _():
        zq_ref[...] = acc.astype(BF16)

    @pl.when((j >= 2) & (j < 4))
    def _():
        zv_ref[...] = acc.astype(BF16)

    @pl.when((j >= 4) & (j < 6))
    def _():
        zo_ref[...] = acc.astype(BF16)

    @pl.when((j >= 6) & (j < 10))
    def _():
        zm_ref[...] = acc.astype(BF16)

    @pl.when(j == 10)
    def _():
        zu_ref[...] = acc


def _inproj(x2d, g, w_main, w_gate):
    m = x2d.shape[0]
    tm, tn = IN_TM, IN_TN
    clampj = lambda lo, n: (lambda i, j: (i, jnp.clip(j - lo, 0, n - 1)))
    out_shapes = (
        jax.ShapeDtypeStruct((m, 2 * W_QK), BF16),
        jax.ShapeDtypeStruct((m, W_V), BF16),
        jax.ShapeDtypeStruct((m, W_V), BF16),
        jax.ShapeDtypeStruct((m, 2 * D_MODEL), BF16),
        jax.ShapeDtypeStruct((m, W_B), F32),
        jax.ShapeDtypeStruct((m, LANES), F32),
    )
    return pl.pallas_call(
        _inproj_body,
        grid=(m // tm, IN_NJ),
        in_specs=[
            pl.BlockSpec((tm, D_MODEL), lambda i, j: (i, 0)),
            pl.BlockSpec((1, D_MODEL), lambda i, j: (0, 0)),
            pl.BlockSpec((D_MODEL, tn), lambda i, j: (0, j)),
            pl.BlockSpec((D_MODEL, LANES), lambda i, j: (0, 0)),
        ],
        out_specs=(
            pl.BlockSpec((tm, tn), clampj(0, 2)),
            pl.BlockSpec((tm, tn), clampj(2, 2)),
            pl.BlockSpec((tm, tn), clampj(4, 2)),
            pl.BlockSpec((tm, tn), clampj(6, 4)),
            pl.BlockSpec((tm, tn), lambda i, j: (i, 0)),
            pl.BlockSpec((tm, LANES), lambda i, j: (i, 0)),
        ),
        out_shape=out_shapes,
        scratch_shapes=[pltpu.VMEM((tm, D_MODEL), BF16)],
        compiler_params=_cparams(("parallel", "arbitrary"), 48),
        name="inproj",
    )(x2d, g, w_main, w_gate)


CONV_TQ = 512
CONV_CW = 512
CONV_HALO = 16


def _conv_body(prev_ref, cur_ref, next_ref, w_ref, s_ref, out_ref):
    i = pl.program_id(1)
    n = pl.num_programs(1)
    pv = prev_ref[...].astype(F32)
    nx = next_ref[...].astype(F32)
    pv = jnp.where(i > 0, pv, 0.0)
    nx = jnp.where(i < n - 1, nx, 0.0)
    ext = jnp.concatenate([pv, cur_ref[...].astype(F32), nx], axis=0)
    rows = ext.shape[0]
    tq = cur_ref.shape[0]
    acc = None
    for j in range(CONV_W):
        d = j - CONV_W // 2
        sh = ext if d == 0 else pltpu.roll(ext, (-d) % rows, 0)
        term = sh[CONV_HALO:CONV_HALO + tq] * w_ref[j:j + 1, :]
        acc = term if acc is None else acc + term
    y = acc * jax.nn.sigmoid(acc) * s_ref[...]
    out_ref[...] = y.astype(BF16)


def _conv_silu(zq3, conv_w8, scale_row):
    b, t, c = zq3.shape
    tq, cw, hl = CONV_TQ, CONV_CW, CONV_HALO
    r = tq // hl
    nh = t // hl
    return pl.pallas_call(
        _conv_body,
        grid=(b, t // tq, c // cw),
        in_specs=[
            pl.BlockSpec((None, hl, cw), lambda bb, i, k: (bb, jnp.maximum(i * r - 1, 0), k)),
            pl.BlockSpec((None, tq, cw), lambda bb, i, k: (bb, i, k)),
            pl.BlockSpec((None, hl, cw), lambda bb, i, k: (bb, jnp.minimum((i + 1) * r, nh - 1), k)),
            pl.BlockSpec((SUBLANES, cw), lambda bb, i, k: (0, k)),
            pl.BlockSpec((1, cw), lambda bb, i, k: (0, k)),
        ],
        out_specs=pl.BlockSpec((None, tq, cw), lambda bb, i, k: (bb, i, k)),
        out_shape=jax.ShapeDtypeStruct((b, t, c), BF16),
        compiler_params=_cparams(("parallel", "parallel", "parallel"), 40),
        name="conv_silu",
    )(zq3, zq3, zq3, conv_w8, scale_row)


GATE_TG = 1024


def _log_sigmoid(x):
    return jnp.minimum(x, 0.0) - jnp.log1p(jnp.exp(-jnp.abs(x)))


def _gates_body(zg_ref, bias_ref, col_ref, row_ref):
    g = zg_ref[...] + bias_ref[...]
    gt = g.T
    h = N_HEADS
    li_f, lf_f = gt[0:h], _log_sigmoid(gt[h:2 * h])
    li_b, lf_b = gt[2 * h:3 * h], _log_sigmoid(gt[3 * h:4 * h])
    tg = gt.shape[1]
    pos = lax.broadcasted_iota(jnp.int32, (h, tg), 1) & (CHUNK - 1)

    def scan(x, op, fill, reverse):
        s = 1
        while s < CHUNK:
            if reverse:
                sh = pltpu.roll(x, tg - s, 1)
                ok = pos < CHUNK - s
            else:
                sh = pltpu.roll(x, s, 1)
                ok = pos >= s
            x = op(x, jnp.where(ok, sh, fill))
            s *= 2
        return x

    b_f = scan(lf_f, jnp.add, 0.0, False)
    b_b = scan(lf_b, jnp.add, 0.0, True)
    w_f = li_f - b_f
    w_b = li_b - b_b
    a_f = scan(w_f, jnp.maximum, -jnp.inf, False)
    a_b = scan(w_b, jnp.maximum, -jnp.inf, True)
    row_ref[...] = jnp.concatenate([w_f, w_b], axis=0)
    pad = jnp.zeros((LANES - 4 * h, tg), F32)
    col_ref[...] = jnp.concatenate([b_f, a_f, b_b, a_b, pad], axis=0).T


def _gate_scans(zg3, bias_row):
    b, t, _ = zg3.shape
    tg = GATE_TG
    return pl.pallas_call(
        _gates_body,
        grid=(b, t // tg),
        in_specs=[
            pl.BlockSpec((None, tg, LANES), lambda bb, i: (bb, i, 0)),
            pl.BlockSpec((1, LANES), lambda bb, i: (0, 0)),
        ],
        out_specs=(
            pl.BlockSpec((None, tg, LANES), lambda bb, i: (bb, i, 0)),
            pl.BlockSpec((None, 2 * N_HEADS, tg), lambda bb, i: (bb, 0, i)),
        ),
        out_shape=(
            jax.ShapeDtypeStruct((b, t, LANES), F32),
            jax.ShapeDtypeStruct((b, 2 * N_HEADS, t), F32),
        ),
        compiler_params=_cparams(("parallel", "parallel"), 32),
        name="gate_scans",
    )(zg3, bias_row)


ST_W = DV + LANES


def _mlstm_dir(qk_ref, v_ref, col_ref, row_ref, h_ref, st_ref, m_ref, d, reverse):
    L = CHUNK
    t_idx = lax.broadcasted_iota(jnp.int32, (L, L), 0)
    s_idx = lax.broadcasted_iota(jnp.int32, (L, L), 1)
    mask = (s_idx >= t_idx) if reverse else (s_idx <= t_idx)
    last = 0 if reverse else L - 1
    ones = jnp.ones((L, LANES), BF16)
    for hd in range(N_HEADS):
        q = qk_ref[:, hd * DK:(hd + 1) * DK]
        k = qk_ref[:, W_QK + hd * DK:W_QK + (hd + 1) * DK]
        v = v_ref[:, hd * DV:(hd + 1) * DV]
        cb = 2 * N_HEADS * d + hd
        bcol = col_ref[:, cb:cb + 1]
        acol = col_ref[:, cb + N_HEADS:cb + N_HEADS + 1]
        wrow = row_ref[N_HEADS * d + hd:N_HEADS * d + hd + 1, :]
        m11 = m_ref[d, hd:hd + 1, 0:1]
        st = st_ref[d, hd]

        mx = jnp.maximum(acol, m11)
        mx_last = mx[last:last + 1, :]
        b_last = bcol[last:last + 1, :]

        s_qk = lax.dot_general(q, k, (((1,), (1,)), ((), ())), preferred_element_type=F32)
        p = jnp.exp(wrow - mx)
        s_mat = jnp.where(mask, s_qk * p, 0.0).astype(BF16)
        w_inter = jnp.exp(m11 - mx)
        qw = (q.astype(F32) * w_inter).astype(BF16)
        lhs = jnp.concatenate([s_mat, qw], axis=1)
        vext = jnp.concatenate([v, ones], axis=1)
        rhs = jnp.concatenate([vext, st.astype(BF16)], axis=0)
        res = jnp.dot(lhs, rhs, preferred_element_type=F32)
        nq = res[:, DV:]
        den = jnp.maximum(jnp.abs(nq), jnp.exp(-(bcol + mx)))
        inv = 1.0 / den
        hout = res[:, :DV] * jnp.concatenate([inv, inv], axis=1)
        h_ref[:, hd * DV:(hd + 1) * DV] = hout.astype(h_ref.dtype)

        w_s = jnp.exp(wrow - mx_last)
        kwt = (k.astype(F32).T * w_s).astype(BF16)
        upd = jnp.dot(kwt, vext, preferred_element_type=F32)
        st_ref[d, hd] = jnp.exp(m11 - mx_last) * st + upd
        m_ref[d, hd:hd + 1, :] = jnp.broadcast_to(b_last + mx_last, (1, LANES))


def _mlstm_body(qkf_ref, qkb_ref, vf_ref, vb_ref, colf_ref, colb_ref, rowf_ref, rowb_ref,
                hf_ref, hb_ref, st_ref, m_ref):
    @pl.when(pl.program_id(1) == 0)
    def _():
        st_ref[...] = jnp.zeros(st_ref.shape, F32)
        m_ref[...] = jnp.zeros(m_ref.shape, F32)

    _mlstm_dir(qkf_ref, vf_ref, colf_ref, rowf_ref, hf_ref, st_ref, m_ref, 0, False)
    _mlstm_dir(qkb_ref, vb_ref, colb_ref, rowb_ref, hb_ref, st_ref, m_ref, 1, True)


def _mlstm(qkc, zv3, gcol, grow):
    b, t, _ = qkc.shape
    L = CHUNK
    nc = t // L
    fwd3 = lambda bb, i: (bb, i, 0)
    bwd3 = lambda bb, i: (bb, nc - 1 - i, 0)
    return pl.pallas_call(
        _mlstm_body,
        grid=(b, nc),
        in_specs=[
            pl.BlockSpec((None, L, 2 * W_QK), fwd3),
            pl.BlockSpec((None, L, 2 * W_QK), bwd3),
            pl.BlockSpec((None, L, W_V), fwd3),
            pl.BlockSpec((None, L, W_V), bwd3),
            pl.BlockSpec((None, L, LANES), fwd3),
            pl.BlockSpec((None, L, LANES), bwd3),
            pl.BlockSpec((None, 2 * N_HEADS, L), lambda bb, i: (bb, 0, i)),
            pl.BlockSpec((None, 2 * N_HEADS, L), lambda bb, i: (bb, 0, nc - 1 - i)),
        ],
        out_specs=(
            pl.BlockSpec((None, L, W_V), fwd3),
            pl.BlockSpec((None, L, W_V), bwd3),
        ),
        out_shape=(
            jax.ShapeDtypeStruct((b, t, W_V), BF16),
            jax.ShapeDtypeStruct((b, t, W_V), BF16),
        ),
        scratch_shapes=[
            pltpu.VMEM((2, N_HEADS, DK, ST_W), F32),
            pltpu.VMEM((2, N_HEADS, LANES), F32),
        ],
        compiler_params=_cparams(("parallel", "arbitrary"), 40),
        name="mlstm",
    )(qkc, qkc, zv3, zv3, gcol, gcol, grow, grow)


FFT_N2 = 128
FFT_R = SUBLANES
FFT1_CB = 256
FFT2_CB = 512


@functools.lru_cache(maxsize=None)
def _fft_tables(t):
    n2 = FFT_N2
    n1 = t // n2
    k1 = np.arange(n1)[:, None].astype(np.float64)
    t1 = np.arange(n1)[None, :].astype(np.float64)
    t2 = np.arange(n2)[:, None, None].astype(np.float64)
    ang = 2.0 * np.pi * (k1 * t1 / n1)[None] + 2.0 * np.pi * (k1[None] * t2 / t)
    tab1 = np.concatenate([np.cos(ang), np.sin(ang)], axis=1) / np.sqrt(n1)
    c = np.arange(GROUP_B)[:, None].astype(np.float64)
    cc = np.arange(GROUP_B)[None, :].astype(np.float64)
    angc = 2.0 * np.pi * c * cc / GROUP_B
    chan = np.concatenate([np.cos(angc), np.sin(angc)], axis=1) / np.sqrt(GROUP_B)
    k2 = np.arange(n2)[:, None].astype(np.float64)
    tt = np.arange(n2)[None, :].astype(np.float64)
    ang2 = 2.0 * np.pi * k2 * tt / n2
    tab2 = np.concatenate([np.cos(ang2), -np.sin(ang2)], axis=1) / np.sqrt(n2)
    return (tab1.astype(np.float32), chan.astype(np.float32), tab2.astype(np.float32))


def _fft1_body(u_ref, tab_ref, e_ref, a_ref, sin_scr, sre_scr, sim_scr):
    n1 = u_ref.shape[0]
    ng = u_ref.shape[2] // LANES
    for g in range(ng):
        sin_scr[g] = u_ref[:, :, g * LANES:(g + 1) * LANES].reshape(n1 * FFT_R, LANES)
    e = e_ref[...]
    for r in range(FFT_R):
        tab = tab_ref[r]
        for g in range(ng):
            x = sin_scr.at[g][pl.ds(r, n1, stride=FFT_R), :]
            uri = jnp.dot(tab, x.astype(BF16), preferred_element_type=F32)
            rr = jnp.dot(uri.astype(BF16), e, preferred_element_type=F32)
            ar = rr[:n1, :LANES] - rr[n1:, LANES:]
            ai = rr[:n1, LANES:] + rr[n1:, :LANES]
            sre_scr.at[g][pl.ds(r, n1, stride=FFT_R), :] = ar
            sim_scr.at[g][pl.ds(r, n1, stride=FFT_R), :] = ai
    for g in range(ng):
        a_ref[:, 0, :, g * LANES:(g + 1) * LANES] = sre_scr[g].reshape(n1, FFT_R, LANES)
        a_ref[:, 1, :, g * LANES:(g + 1) * LANES] = sim_scr[g].reshape(n1, FFT_R, LANES)


def _fft2_body(a_ref, tab_ref, y_ref, s_scr):
    n2 = a_ref.shape[2]
    ng = a_ref.shape[3] // LANES
    tab = tab_ref[...]
    for r in range(FFT_R):
        bri = a_ref[r].reshape(2 * n2, a_ref.shape[3]).astype(BF16)
        y = jnp.dot(tab, bri, preferred_element_type=F32)
        for g in range(ng):
            s_scr.at[g][pl.ds(r, n2, stride=FFT_R), :] = y[:, g * LANES:(g + 1) * LANES]
    for g in range(ng):
        y_ref[:, :, g * LANES:(g + 1) * LANES] = s_scr[g].reshape(n2, FFT_R, LANES)


def _fourier(zu3):
    b, t, c = zu3.shape
    n2 = FFT_N2
    n1 = t // n2
    tab1, chan, tab2 = _fft_tables(t)
    tab1 = jnp.asarray(tab1, BF16)
    chan = jnp.asarray(chan, BF16)
    tab2 = jnp.asarray(tab2, BF16)
    u4 = zu3.reshape(b, n1, n2, c)
    cb1, cb2, r = FFT1_CB, FFT2_CB, FFT_R
    a = pl.pallas_call(
        _fft1_body,
        grid=(b, n2 // r, c // cb1),
        in_specs=[
            pl.BlockSpec((None, n1, r, cb1), lambda bb, j, k: (bb, 0, j, k)),
            pl.BlockSpec((r, 2 * n1, n1), lambda bb, j, k: (j, 0, 0)),
            pl.BlockSpec((GROUP_B, 2 * GROUP_B), lambda bb, j, k: (0, 0)),
        ],
        out_specs=pl.BlockSpec((None, n1, 2, r, cb1), lambda bb, j, k: (bb, 0, 0, j, k)),
        out_shape=jax.ShapeDtypeStruct((b, n1, 2, n2, c), F32),
        scratch_shapes=[pltpu.VMEM((cb1 // LANES, n1 * r, LANES), F32)] * 3,
        compiler_params=_cparams(("parallel", "parallel", "parallel"), 40),
        name="fft_stage1",
    )(u4, tab1, chan)
    y = pl.pallas_call(
        _fft2_body,
        grid=(b, n1 // r, c // cb2),
        in_specs=[
            pl.BlockSpec((None, r, 2, n2, cb2), lambda bb, j, k: (bb, j, 0, 0, k)),
            pl.BlockSpec((n2, 2 * n2), lambda bb, j, k: (0, 0)),
        ],
        out_specs=pl.BlockSpec((None, n2, r, cb2), lambda bb, j, k: (bb, 0, j, k)),
        out_shape=jax.ShapeDtypeStruct((b, n2, n1, c), F32),
        scratch_shapes=[pltpu.VMEM((cb2 // LANES, n2 * r, LANES), F32)],
        compiler_params=_cparams(("parallel", "parallel", "parallel"), 40),
        name="fft_stage2",
    )(a, tab2)
    return y.reshape(b, t, c)


OUT_TM = 256


def _rowsum_lanes(x, ones_bf16):
    hi = x.astype(BF16)
    lo = (x - hi.astype(F32)).astype(BF16)
    return (jnp.dot(hi, ones_bf16, preferred_element_type=F32)
            + jnp.dot(lo, ones_bf16, preferred_element_type=F32))


def _outproj_body(hf_ref, hb_ref, o_ref, gh_ref, fb_ref, ma_ref, mb_ref, bma_ref, bmb_ref, x_ref,
                  wa_ref, wb_ref, wo_ref, gpost_ref, out_ref):
    ones = jnp.ones((DV, LANES), BF16)
    ha = hf_ref[...].astype(F32) + hb_ref[...].astype(F32)
    parts = []
    for hd in range(N_HEADS):
        hh = ha[:, hd * DV:(hd + 1) * DV]
        ssum = _rowsum_lanes(hh * hh, ones)
        inv = lax.rsqrt(ssum * (1.0 / DV) + EPS)
        parts.append(hh * jnp.concatenate([inv, inv], axis=1))
    han = jnp.concatenate(parts, axis=1) * gh_ref[...]
    han = han * jax.nn.sigmoid(o_ref[...].astype(F32))
    ya = jnp.dot(han.astype(BF16), wa_ref[...], preferred_element_type=F32)
    yb = jnp.dot(fb_ref[...].astype(BF16), wb_ref[...], preferred_element_type=F32)
    ga = jax.nn.sigmoid(ma_ref[...].astype(F32) + bma_ref[...])
    gb = jax.nn.sigmoid(mb_ref[...].astype(F32) + bmb_ref[...])
    mixin = (ga * ya + gb * yb).astype(BF16)
    mix = jnp.dot(mixin, wo_ref[...], preferred_element_type=F32)
    ms = jnp.mean(mix * mix, axis=-1, keepdims=True)
    out_ref[...] = x_ref[...] + mix * lax.rsqrt(ms + EPS) * gpost_ref[...]


def _outproj(hf, hb, zo, g_head, fb, zm, b_merge, x2d, wa, wb, wo, g_post):
    m = x2d.shape[0]
    tm = OUT_TM
    row = lambda i: (i, 0)
    const = lambda i: (0, 0)
    resident = functools.partial(pl.BlockSpec, index_map=const, pipeline_mode=pl.Buffered(1))
    return pl.pallas_call(
        _outproj_body,
        grid=(m // tm,),
        in_specs=[
            pl.BlockSpec((tm, W_V), row),
            pl.BlockSpec((tm, W_V), row),
            pl.BlockSpec((tm, W_V), row),
            pl.BlockSpec((1, W_V), const),
            pl.BlockSpec((tm, W_B), row),
            pl.BlockSpec((tm, D_MODEL), lambda i: (i, 0)),
            pl.BlockSpec((tm, D_MODEL), lambda i: (i, 1)),
            pl.BlockSpec((1, D_MODEL), lambda i: (0, 0)),
            pl.BlockSpec((1, D_MODEL), lambda i: (0, 1)),
            pl.BlockSpec((tm, D_MODEL), row),
            resident((W_V, D_MODEL)),
            resident((W_B, D_MODEL)),
            resident((D_MODEL, D_MODEL)),
            pl.BlockSpec((1, D_MODEL), const),
        ],
        out_specs=pl.BlockSpec((tm, D_MODEL), row),
        out_shape=jax.ShapeDtypeStruct((m, D_MODEL), F32),
        compiler_params=_cparams(("parallel",), 56),
        name="outproj",
    )(hf, hb, zo, g_head, fb, zm, zm, b_merge, b_merge, x2d, wa, wb, wo, g_post)


FFN_TM = 512
FFN_TF = 512


def _ffn_body(x_ref, gpre_ref, wg_ref, wu_ref, wo_ref, gpost_ref, out_ref, h_scr, acc_scr):
    j = pl.program_id(1)

    @pl.when(j == 0)
    def _():
        x = x_ref[...]
        ms = jnp.mean(x * x, axis=-1, keepdims=True)
        h_scr[...] = (x * lax.rsqrt(ms + EPS) * gpre_ref[...]).astype(BF16)
        acc_scr[...] = jnp.zeros(acc_scr.shape, F32)

    h = h_scr[...]
    g = jnp.dot(h, wg_ref[...], preferred_element_type=F32)
    u = jnp.dot(h, wu_ref[...], preferred_element_type=F32)
    act = (g * jax.nn.sigmoid(g) * u).astype(BF16)
    acc_scr[...] += jnp.dot(act, wo_ref[...], preferred_element_type=F32)

    @pl.when(j == pl.num_programs(1) - 1)
    def _():
        ff = acc_scr[...]
        ms = jnp.mean(ff * ff, axis=-1, keepdims=True)
        out_ref[...] = x_ref[...] + ff * lax.rsqrt(ms + EPS) * gpost_ref[...]


def _ffn(x2d, g_pre, w_in, w_out, g_post):
    m = x2d.shape[0]
    tm, tf = FFN_TM, FFN_TF
    nf = D_FF // tf
    return pl.pallas_call(
        _ffn_body,
        grid=(m // tm, nf),
        in_specs=[
            pl.BlockSpec((tm, D_MODEL), lambda i, j: (i, 0)),
            pl.BlockSpec((1, D_MODEL), lambda i, j: (0, 0)),
            pl.BlockSpec((D_MODEL, tf), lambda i, j: (0, j)),
            pl.BlockSpec((D_MODEL, tf), lambda i, j: (0, j + nf)),
            pl.BlockSpec((tf, D_MODEL), lambda i, j: (j, 0)),
            pl.BlockSpec((1, D_MODEL), lambda i, j: (0, 0)),
        ],
        out_specs=pl.BlockSpec((tm, D_MODEL), lambda i, j: (i, 0)),
        out_shape=jax.ShapeDtypeStruct((m, D_MODEL), F32),
        scratch_shapes=[pltpu.VMEM((tm, D_MODEL), BF16), pltpu.VMEM((tm, D_MODEL), F32)],
        compiler_params=_cparams(("parallel", "arbitrary"), 48),
        name="ffn",
    )(x2d, g_pre, w_in, w_in, w_out, g_post)


def _prep_layer(w_in, conv_w, b_gates, g_head, w_a_out, w_b_out, b_merge, w_out,
                w_ffn_in, w_ffn_out):
    w_main = jnp.concatenate([w_in[:, :OFF_G], w_in[:, OFF_M:], w_in[:, OFF_B:OFF_M]], axis=1).astype(BF16)
    w_gate = jnp.pad(w_in[:, OFF_G:OFF_B], ((0, 0), (0, LANES - N_GATES))).astype(BF16)
    conv_w8 = jnp.pad(conv_w, ((0, SUBLANES - CONV_W), (0, 0)))
    scale_row = jnp.concatenate([jnp.ones((1, W_QK), F32), jnp.full((1, W_QK), DK ** -0.5, F32)], axis=1)
    bias_row = jnp.pad(b_gates[None, :], ((0, 0), (0, LANES - N_GATES)))
    return dict(
        w_main=w_main, w_gate=w_gate, conv_w8=conv_w8, scale_row=scale_row, bias_row=bias_row,
        g_head=g_head[None, :], wa=w_a_out.astype(BF16), wb=w_b_out.astype(BF16),
        b_merge=b_merge[None, :], wo=w_out.astype(BF16),
        w_ffn_in=w_ffn_in.astype(BF16), w_ffn_out=w_ffn_out.astype(BF16),
    )


def _layer(x, p, g_pre_mix, g_post_mix, g_pre_ffn, g_post_ffn):
    b, t, d = x.shape
    m = b * t
    x2d = x.reshape(m, d)
    zq, zv, zo, zm, zu, zg = _inproj(x2d, g_pre_mix[None, :], p["w_main"], p["w_gate"])
    qkc = _conv_silu(zq.reshape(b, t, 2 * W_QK), p["conv_w8"], p["scale_row"])
    gcol, grow = _gate_scans(zg.reshape(b, t, LANES), p["bias_row"])
    hf, hb = _mlstm(qkc, zv.reshape(b, t, W_V), gcol, grow)
    fb = _fourier(zu.reshape(b, t, W_B))
    x1 = _outproj(hf.reshape(m, W_V), hb.reshape(m, W_V), zo, p["g_head"], fb.reshape(m, W_B), zm,
                  p["b_merge"], x2d, p["wa"], p["wb"], p["wo"], g_post_mix[None, :])
    y = _ffn(x1, g_pre_ffn[None, :], p["w_ffn_in"], p["w_ffn_out"], g_post_ffn[None, :])
    return y.reshape(b, t, d)


def kernel(x_prompt, x_sample, g_pre_mix, w_in, conv_w, b_gates, g_head, w_a_out, w_b_out, b_merge,
           w_out, g_post_mix, g_pre_ffn, w_ffn_in, w_ffn_out, g_post_ffn):
    depth = w_in.shape[0]
    layers = [
        _prep_layer(w_in[l], conv_w[l], b_gates[l], g_head[l], w_a_out[l], w_b_out[l], b_merge[l],
                    w_out[l], w_ffn_in[l], w_ffn_out[l])
        for l in range(depth)
    ]

    def trunk(x):
        for l in range(depth):
            x = _layer(x, layers[l], g_pre_mix[l], g_post_mix[l], g_pre_ffn[l], g_post_ffn[l])
        return x

    return (trunk(x_prompt), trunk(x_sample))
```

```python
import functools

import numpy as np
import jax
import jax.numpy as jnp
from jax import lax
from jax.experimental import pallas as pl
from jax.experimental.pallas import tpu as pltpu

F32 = jnp.float32
BF16 = jnp.bfloat16

D_MODEL = 2048
N_HEADS = 8
DK = 128
DV = 256
W_QK = N_HEADS * DK
W_V = N_HEADS * DV
N_GATES = 4 * N_HEADS
CONV_W = 5
CHUNK = 128
N_GROUPS_B = 8
GROUP_B = 128
W_B = N_GROUPS_B * GROUP_B
D_FF = 5632
OFF_V = 2 * W_QK
OFF_O = OFF_V + W_V
OFF_G = OFF_O + W_V
OFF_B = OFF_G + N_GATES
OFF_M = OFF_B + W_B
EPS = 1e-6

LANES = 128
SUBLANES = 8
MIB = 1024 * 1024


def _cparams(sem, vmem_mib):
    return pltpu.CompilerParams(dimension_semantics=sem, vmem_limit_bytes=vmem_mib * MIB)


IN_TM = 1024
IN_TN = 1024
ZC_QK = 0
ZC_V = ZC_QK + 2 * W_QK
ZC_O = ZC_V + W_V
ZC_MA = ZC_O + W_V
ZC_MB = ZC_MA + D_MODEL
ZC_U = ZC_MB + D_MODEL
Z_W = ZC_U + W_B
IN_NJ = Z_W // IN_TN
assert IN_TN == W_B and ZC_U == (IN_NJ - 1) * IN_TN


def _inproj_body(x_ref, g_ref, w_ref, wg_ref, z_ref, zu_ref, zg_ref, h_scr):
    j = pl.program_id(1)

    @pl.when(j == 0)
    def _():
        x = x_ref[...]
        ms = jnp.mean(x * x, axis=-1, keepdims=True)
        h = (x * lax.rsqrt(ms + EPS) * g_ref[...]).astype(BF16)
        h_scr[...] = h
        zg_ref[...] = jnp.dot(h, wg_ref[...], preferred_element_type=F32)

    acc = jnp.dot(h_scr[...], w_ref[...], preferred_element_type=F32)
    z_ref[...] = acc.astype(BF16)

    @pl.when(j == IN_NJ - 1)
    def _():
        zu_ref[...] = acc


def _inproj(x2d, g, w_main, w_gate):
    m = x2d.shape[0]
    tm, tn = IN_TM, IN_TN
    out_shapes = (
        jax.ShapeDtypeStruct((m, Z_W), BF16),
        jax.ShapeDtypeStruct((m, W_B), F32),
        jax.ShapeDtypeStruct((m, LANES), F32),
    )
    return pl.pallas_call(
        _inproj_body,
        grid=(m // tm, IN_NJ),
        in_specs=[
            pl.BlockSpec((tm, D_MODEL), lambda i, j: (i, 0)),
            pl.BlockSpec((1, D_MODEL), lambda i, j: (0, 0)),
            pl.BlockSpec((D_MODEL, tn), lambda i, j: (0, j)),
            pl.BlockSpec((D_MODEL, LANES), lambda i, j: (0, 0)),
        ],
        out_specs=(
            pl.BlockSpec((tm, tn), lambda i, j: (i, j)),
            pl.BlockSpec((tm, W_B), lambda i, j: (i, 0)),
            pl.BlockSpec((tm, LANES), lambda i, j: (i, 0)),
        ),
        out_shape=out_shapes,
        scratch_shapes=[pltpu.VMEM((tm, D_MODEL), BF16)],
        compiler_params=_cparams(("parallel", "arbitrary"), 56),
        name="inproj",
    )(x2d, g, w_main, w_gate)


CONV_TQ = 512
CONV_CW = 512
CONV_HALO = 16
CONV_NQ = W_QK // CONV_CW


def _conv_body(prev_ref, cur_ref, next_ref, w_ref, out_ref, *, keys):
    i = pl.program_id(1)
    n = pl.num_programs(1)
    tq = cur_ref.shape[0]
    hl = CONV_HALO
    pv = jnp.where(i > 0, prev_ref[...].astype(F32), 0.0)
    nx = jnp.where(i < n - 1, next_ref[...].astype(F32), 0.0)
    ext = jnp.concatenate([pv, cur_ref[...].astype(F32), nx], axis=0)
    rows = tq + 2 * hl
    acc = None
    for j in range(CONV_W):
        d = j - CONV_W // 2
        sh = ext if d == 0 else pltpu.roll(ext, (-d) % rows, 0)
        term = sh[hl:hl + tq] * w_ref[j:j + 1, :]
        acc = term if acc is None else acc + term
    y = acc * jax.nn.sigmoid(acc)
    if keys:
        out_ref[...] = (y * (DK ** -0.5)).T.astype(BF16)
    else:
        out_ref[...] = y.astype(BF16)


def _conv_silu(z3, conv_w8, keys):
    b, t, _ = z3.shape
    tq, cw, hl = CONV_TQ, CONV_CW, CONV_HALO
    r = tq // hl
    nh = t // hl
    c0 = CONV_NQ if keys else 0
    if keys:
        out_spec = pl.BlockSpec((None, cw, tq), lambda bb, i, k: (bb, k, i))
        out_shape = jax.ShapeDtypeStruct((b, W_QK, t), BF16)
    else:
        out_spec = pl.BlockSpec((None, tq, cw), lambda bb, i, k: (bb, i, k))
        out_shape = jax.ShapeDtypeStruct((b, t, W_QK), BF16)
    return pl.pallas_call(
        functools.partial(_conv_body, keys=keys),
        grid=(b, t // tq, CONV_NQ),
        in_specs=[
            pl.BlockSpec((None, hl, cw), lambda bb, i, k: (bb, jnp.maximum(i * r - 1, 0), c0 + k)),
            pl.BlockSpec((None, tq, cw), lambda bb, i, k: (bb, i, c0 + k)),
            pl.BlockSpec((None, hl, cw), lambda bb, i, k: (bb, jnp.minimum((i + 1) * r, nh - 1), c0 + k)),
            pl.BlockSpec((SUBLANES, cw), lambda bb, i, k: (0, c0 + k)),
        ],
        out_specs=out_spec,
        out_shape=out_shape,
        compiler_params=_cparams(("parallel", "parallel", "parallel"), 40),
        name="conv_silu_k" if keys else "conv_silu_q",
    )(z3, z3, z3, conv_w8)


GATE_TG = 1024


def _log_sigmoid(x):
    return jnp.minimum(x, 0.0) - jnp.log1p(jnp.exp(-jnp.abs(x)))


def _gates_body(zg_ref, bias_ref, col_ref, row_ref):
    g = zg_ref[...] + bias_ref[...]
    gt = g.T
    h = N_HEADS
    li_f, lf_f = gt[0:h], _log_sigmoid(gt[h:2 * h])
    li_b, lf_b = gt[2 * h:3 * h], _log_sigmoid(gt[3 * h:4 * h])
    tg = gt.shape[1]
    pos = lax.broadcasted_iota(jnp.int32, (h, tg), 1) & (CHUNK - 1)

    def scan(x, op, fill, reverse):
        s = 1
        while s < CHUNK:
            if reverse:
                sh = pltpu.roll(x, tg - s, 1)
                ok = pos < CHUNK - s
            else:
                sh = pltpu.roll(x, s, 1)
                ok = pos >= s
            x = op(x, jnp.where(ok, sh, fill))
            s *= 2
        return x

    b_f = scan(lf_f, jnp.add, 0.0, False)
    b_b = scan(lf_b, jnp.add, 0.0, True)
    w_f = li_f - b_f
    w_b = li_b - b_b
    a_f = scan(w_f, jnp.maximum, -jnp.inf, False)
    a_b = scan(w_b, jnp.maximum, -jnp.inf, True)
    row_ref[...] = jnp.concatenate([w_f, w_b], axis=0)
    pad = jnp.zeros((LANES - 4 * h, tg), F32)
    col_ref[...] = jnp.concatenate([b_f, a_f, b_b, a_b, pad], axis=0).T


def _gate_scans(zg3, bias_row):
    b, t, _ = zg3.shape
    tg = GATE_TG
    return pl.pallas_call(
        _gates_body,
        grid=(b, t // tg),
        in_specs=[
            pl.BlockSpec((None, tg, LANES), lambda bb, i: (bb, i, 0)),
            pl.BlockSpec((1, LANES), lambda bb, i: (0, 0)),
        ],
        out_specs=(
            pl.BlockSpec((None, tg, LANES), lambda bb, i: (bb, i, 0)),
            pl.BlockSpec((None, 2 * N_HEADS, tg), lambda bb, i: (bb, 0, i)),
        ),
        out_shape=(
            jax.ShapeDtypeStruct((b, t, LANES), F32),
            jax.ShapeDtypeStruct((b, 2 * N_HEADS, t), F32),
        ),
        compiler_params=_cparams(("parallel", "parallel"), 32),
        name="gate_scans",
    )(zg3, bias_row)


ST_W = DV + LANES


def _mlstm_dir(q_ref, kt_ref, v_ref, col_ref, row_ref, h_ref, st_ref, m_ref, d, reverse):
    L = CHUNK
    t_idx = lax.broadcasted_iota(jnp.int32, (L, L), 0)
    s_idx = lax.broadcasted_iota(jnp.int32, (L, L), 1)
    mask = (s_idx >= t_idx) if reverse else (s_idx <= t_idx)
    last = 0 if reverse else L - 1
    ones = jnp.ones((L, LANES), BF16)
    for hd in range(N_HEADS):
        q = q_ref[:, hd * DK:(hd + 1) * DK]
        kt = kt_ref[hd * DK:(hd + 1) * DK, :]
        v = v_ref[:, hd * DV:(hd + 1) * DV]
        cb = 2 * N_HEADS * d + hd
        bcol = col_ref[:, cb:cb + 1]
        acol = col_ref[:, cb + N_HEADS:cb + N_HEADS + 1]
        wrow = row_ref[N_HEADS * d + hd:N_HEADS * d + hd + 1, :]
        m11 = m_ref[d, hd:hd + 1, 0:1]
        st = st_ref[d, hd]

        mx = jnp.maximum(acol, m11)
        mx_last = mx[last:last + 1, :]
        b_last = bcol[last:last + 1, :]

        s_qk = jnp.dot(q, kt, preferred_element_type=F32)
        mxb = jnp.broadcast_to(mx, (L, LANES))
        p = jnp.exp(wrow - mxb)
        s_mat = jnp.where(mask, s_qk * p, 0.0).astype(BF16)
        qw = (q.astype(F32) * jnp.exp(m11 - mxb)).astype(BF16)
        lhs = jnp.concatenate([s_mat, qw], axis=1)
        vext = jnp.concatenate([v, ones], axis=1)
        rhs = jnp.concatenate([vext, st.astype(BF16)], axis=0)
        res = jnp.dot(lhs, rhs, preferred_element_type=F32)
        nq = res[:, DV:]
        den = jnp.maximum(jnp.abs(nq), jnp.exp(-(bcol + mx)))
        inv = 1.0 / den
        hout = res[:, :DV] * jnp.concatenate([inv, inv], axis=1)
        h_ref[:, hd * DV:(hd + 1) * DV] = hout.astype(h_ref.dtype)

        w_s = jnp.exp(wrow - mx_last)
        kwt = (kt.astype(F32) * w_s).astype(BF16)
        upd = jnp.dot(kwt, vext, preferred_element_type=F32)
        st_ref[d, hd] = jnp.exp(m11 - mx_last) * st + upd
        m_ref[d, hd:hd + 1, :] = jnp.broadcast_to(b_last + mx_last, (1, LANES))


def _mlstm_body(qf_ref, qb_ref, ktf_ref, ktb_ref, vf_ref, vb_ref, colf_ref, colb_ref, rowf_ref, rowb_ref,
                hf_ref, hb_ref, st_ref, m_ref):
    @pl.when(pl.program_id(1) == 0)
    def _():
        st_ref[...] = jnp.zeros(st_ref.shape, F32)
        m_ref[...] = jnp.zeros(m_ref.shape, F32)

    _mlstm_dir(qf_ref, ktf_ref, vf_ref, colf_ref, rowf_ref, hf_ref, st_ref, m_ref, 0, False)
    _mlstm_dir(qb_ref, ktb_ref, vb_ref, colb_ref, rowb_ref, hb_ref, st_ref, m_ref, 1, True)


def _mlstm(qc, ktc, z3, gcol, grow):
    b, t, _ = qc.shape
    L = CHUNK
    nc = t // L
    vcol = ZC_V // W_V
    fwd3 = lambda bb, i: (bb, i, 0)
    bwd3 = lambda bb, i: (bb, nc - 1 - i, 0)
    return pl.pallas_call(
        _mlstm_body,
        grid=(b, nc),
        in_specs=[
            pl.BlockSpec((None, L, W_QK), fwd3),
            pl.BlockSpec((None, L, W_QK), bwd3),
            pl.BlockSpec((None, W_QK, L), lambda bb, i: (bb, 0, i)),
            pl.BlockSpec((None, W_QK, L), lambda bb, i: (bb, 0, nc - 1 - i)),
            pl.BlockSpec((None, L, W_V), lambda bb, i: (bb, i, vcol)),
            pl.BlockSpec((None, L, W_V), lambda bb, i: (bb, nc - 1 - i, vcol)),
            pl.BlockSpec((None, L, LANES), fwd3),
            pl.BlockSpec((None, L, LANES), bwd3),
            pl.BlockSpec((None, 2 * N_HEADS, L), lambda bb, i: (bb, 0, i)),
            pl.BlockSpec((None, 2 * N_HEADS, L), lambda bb, i: (bb, 0, nc - 1 - i)),
        ],
        out_specs=(
            pl.BlockSpec((None, L, W_V), fwd3),
            pl.BlockSpec((None, L, W_V), bwd3),
        ),
        out_shape=(
            jax.ShapeDtypeStruct((b, t, W_V), BF16),
            jax.ShapeDtypeStruct((b, t, W_V), BF16),
        ),
        scratch_shapes=[
            pltpu.VMEM((2, N_HEADS, DK, ST_W), F32),
            pltpu.VMEM((2, N_HEADS, LANES), F32),
        ],
        compiler_params=_cparams(("parallel", "arbitrary"), 40),
        name="mlstm",
    )(qc, qc, ktc, ktc, z3, z3, gcol, gcol, grow, grow)


FFT_N2 = 128
FFT_R = SUBLANES
FFT1_CB = W_B
FFT2_CB = 512


@functools.lru_cache(maxsize=None)
def _fft_tables(t):
    n2 = FFT_N2
    n1 = t // n2
    k1 = np.arange(n1)[:, None].astype(np.float64)
    t1 = np.arange(n1)[None, :].astype(np.float64)
    t2 = np.arange(n2)[:, None, None].astype(np.float64)
    ang = 2.0 * np.pi * (k1 * t1 / n1)[None] + 2.0 * np.pi * (k1[None] * t2 / t)
    tab1 = np.concatenate([np.cos(ang), np.sin(ang)], axis=1) / np.sqrt(n1)
    c = np.arange(GROUP_B)[:, None].astype(np.float64)
    cc = np.arange(GROUP_B)[None, :].astype(np.float64)
    angc = 2.0 * np.pi * c * cc / GROUP_B
    chan = np.concatenate([np.cos(angc), np.sin(angc)], axis=1) / np.sqrt(GROUP_B)
    k2 = np.arange(n2)[:, None].astype(np.float64)
    tt = np.arange(n2)[None, :].astype(np.float64)
    ang2 = 2.0 * np.pi * k2 * tt / n2
    tab2 = np.concatenate([np.cos(ang2), -np.sin(ang2)], axis=1) / np.sqrt(n2)
    return tuple(np.asarray(a.astype(np.float32)).astype(BF16) for a in (tab1, chan, tab2))


def _fft1_body(u_ref, tab_ref, e_ref, a_ref, sin_scr, uri_scr, sre_scr, sim_scr):
    n1 = u_ref.shape[0]
    ng = u_ref.shape[2] // LANES
    for g in range(ng):
        sin_scr[g] = u_ref[:, :, g * LANES:(g + 1) * LANES].reshape(n1 * FFT_R, LANES)
    for r in range(FFT_R):
        x = jnp.concatenate(
            [sin_scr.at[g][pl.ds(r, n1, stride=FFT_R), :] for g in range(ng)], axis=1)
        uri = jnp.dot(tab_ref[r], x.astype(BF16), preferred_element_type=F32)
        uri_scr[r] = uri.astype(BF16)
    e = e_ref[...]
    for g in range(ng):
        lhs = uri_scr[:, :, g * LANES:(g + 1) * LANES].reshape(FFT_R * 2 * n1, LANES)
        rr = jnp.dot(lhs, e, preferred_element_type=F32).reshape(FFT_R, 2 * n1, 2 * LANES)
        for r in range(FFT_R):
            ar = rr[r, :n1, :LANES] - rr[r, n1:, LANES:]
            ai = rr[r, :n1, LANES:] + rr[r, n1:, :LANES]
            sre_scr.at[g][pl.ds(r, n1, stride=FFT_R), :] = ar
            sim_scr.at[g][pl.ds(r, n1, stride=FFT_R), :] = ai
    for g in range(ng):
        a_ref[:, 0, :, g * LANES:(g + 1) * LANES] = sre_scr[g].reshape(n1, FFT_R, LANES)
        a_ref[:, 1, :, g * LANES:(g + 1) * LANES] = sim_scr[g].reshape(n1, FFT_R, LANES)


def _fft2_body(a_ref, tab_ref, y_ref, s_scr):
    n2 = a_ref.shape[2]
    ng = a_ref.shape[3] // LANES
    tab = tab_ref[...]
    for r in range(FFT_R):
        bri = a_ref[r].reshape(2 * n2, a_ref.shape[3]).astype(BF16)
        y = jnp.dot(tab, bri, preferred_element_type=F32)
        for g in range(ng):
            s_scr.at[g][pl.ds(r, n2, stride=FFT_R), :] = y[:, g * LANES:(g + 1) * LANES]
    for g in range(ng):
        y_ref[:, :, g * LANES:(g + 1) * LANES] = s_scr[g].reshape(n2, FFT_R, LANES)


def _fourier(zu3):
    b, t, c = zu3.shape
    n2 = FFT_N2
    n1 = t // n2
    tab1, chan, tab2 = _fft_tables(t)
    tab1, chan, tab2 = jnp.asarray(tab1), jnp.asarray(chan), jnp.asarray(tab2)
    u4 = zu3.reshape(b, n1, n2, c)
    cb1, cb2, r = FFT1_CB, FFT2_CB, FFT_R
    a = pl.pallas_call(
        _fft1_body,
        grid=(b, n2 // r, c // cb1),
        in_specs=[
            pl.BlockSpec((None, n1, r, cb1), lambda bb, j, k: (bb, 0, j, k)),
            pl.BlockSpec((r, 2 * n1, n1), lambda bb, j, k: (j, 0, 0)),
            pl.BlockSpec((GROUP_B, 2 * GROUP_B), lambda bb, j, k: (0, 0)),
        ],
        out_specs=pl.BlockSpec((None, n1, 2, r, cb1), lambda bb, j, k: (bb, 0, 0, j, k)),
        out_shape=jax.ShapeDtypeStruct((b, n1, 2, n2, c), F32),
        scratch_shapes=[
            pltpu.VMEM((cb1 // LANES, n1 * r, LANES), F32),
            pltpu.VMEM((r, 2 * n1, cb1), BF16),
            pltpu.VMEM((cb1 // LANES, n1 * r, LANES), F32),
            pltpu.VMEM((cb1 // LANES, n1 * r, LANES), F32),
        ],
        compiler_params=_cparams(("parallel", "parallel", "parallel"), 56),
        name="fft_stage1",
    )(u4, tab1, chan)
    y = pl.pallas_call(
        _fft2_body,
        grid=(b, n1 // r, c // cb2),
        in_specs=[
            pl.BlockSpec((None, r, 2, n2, cb2), lambda bb, j, k: (bb, j, 0, 0, k)),
            pl.BlockSpec((n2, 2 * n2), lambda bb, j, k: (0, 0)),
        ],
        out_specs=pl.BlockSpec((None, n2, r, cb2), lambda bb, j, k: (bb, 0, j, k)),
        out_shape=jax.ShapeDtypeStruct((b, n2, n1, c), F32),
        scratch_shapes=[pltpu.VMEM((cb2 // LANES, n2 * r, LANES), F32)],
        compiler_params=_cparams(("parallel", "parallel", "parallel"), 40),
        name="fft_stage2",
    )(a, tab2)
    return y.reshape(b, t, c)


OUT_TM = 256


def _rowsum_lanes(x, ones_bf16):
    return jnp.dot(x.astype(BF16), ones_bf16, preferred_element_type=F32)


def _outproj_body(hf_ref, hb_ref, o_ref, gh_ref, fb_ref, ma_ref, mb_ref, bma_ref, bmb_ref, x_ref,
                  wa_ref, wb_ref, wo_ref, gpost_ref, out_ref):
    ones = jnp.ones((DV, LANES), BF16)
    ha = hf_ref[...].astype(F32) + hb_ref[...].astype(F32)
    parts = []
    for hd in range(N_HEADS):
        hh = ha[:, hd * DV:(hd + 1) * DV]
        ssum = _rowsum_lanes(hh * hh, ones)
        inv = lax.rsqrt(ssum * (1.0 / DV) + EPS)
        parts.append(hh * jnp.concatenate([inv, inv], axis=1))
    han = jnp.concatenate(parts, axis=1) * gh_ref[...]
    han = han * jax.nn.sigmoid(o_ref[...].astype(F32))
    ya = jnp.dot(han.astype(BF16), wa_ref[...], preferred_element_type=F32)
    yb = jnp.dot(fb_ref[...].astype(BF16), wb_ref[...], preferred_element_type=F32)
    ga = jax.nn.sigmoid(ma_ref[...].astype(F32) + bma_ref[...])
    gb = jax.nn.sigmoid(mb_ref[...].astype(F32) + bmb_ref[...])
    mixin = (ga * ya + gb * yb).astype(BF16)
    mix = jnp.dot(mixin, wo_ref[...], preferred_element_type=F32)
    ms = jnp.mean(mix * mix, axis=-1, keepdims=True)
    out_ref[...] = x_ref[...] + mix * lax.rsqrt(ms + EPS) * gpost_ref[...]


def _outproj(hf, hb, z, g_head, fb, b_merge, x2d, wa, wb, wo, g_post):
    m = x2d.shape[0]
    tm = OUT_TM
    row = lambda i: (i, 0)
    const = lambda i: (0, 0)
    resident = functools.partial(pl.BlockSpec, index_map=const, pipeline_mode=pl.Buffered(1))
    return pl.pallas_call(
        _outproj_body,
        grid=(m // tm,),
        in_specs=[
            pl.BlockSpec((tm, W_V), row),
            pl.BlockSpec((tm, W_V), row),
            pl.BlockSpec((tm, W_V), lambda i: (i, ZC_O // W_V)),
            pl.BlockSpec((1, W_V), const),
            pl.BlockSpec((tm, W_B), row),
            pl.BlockSpec((tm, D_MODEL), lambda i: (i, ZC_MA // D_MODEL)),
            pl.BlockSpec((tm, D_MODEL), lambda i: (i, ZC_MB // D_MODEL)),
            pl.BlockSpec((1, D_MODEL), lambda i: (0, 0)),
            pl.BlockSpec((1, D_MODEL), lambda i: (0, 1)),
            pl.BlockSpec((tm, D_MODEL), row),
            resident((W_V, D_MODEL)),
            resident((W_B, D_MODEL)),
            resident((D_MODEL, D_MODEL)),
            pl.BlockSpec((1, D_MODEL), const),
        ],
        out_specs=pl.BlockSpec((tm, D_MODEL), row),
        out_shape=jax.ShapeDtypeStruct((m, D_MODEL), F32),
        compiler_params=_cparams(("parallel",), 56),
        name="outproj",
    )(hf, hb, z, g_head, fb, z, z, b_merge, b_merge, x2d, wa, wb, wo, g_post)


FFN_TM = 1024
FFN_TF = 256


def _ffn_body(x_ref, gpre_ref, wg_ref, wu_ref, wo_ref, gpost_ref, out_ref, h_scr):
    j = pl.program_id(1)

    @pl.when(j == 0)
    def _():
        x = x_ref[...]
        ms = jnp.mean(x * x, axis=-1, keepdims=True)
        h_scr[...] = (x * lax.rsqrt(ms + EPS) * gpre_ref[...]).astype(BF16)
        out_ref[...] = jnp.zeros(out_ref.shape, F32)

    h = h_scr[...]
    g = jnp.dot(h, wg_ref[...], preferred_element_type=F32)
    u = jnp.dot(h, wu_ref[...], preferred_element_type=F32)
    act = (g * jax.nn.sigmoid(g) * u).astype(BF16)
    out_ref[...] += jnp.dot(act, wo_ref[...], preferred_element_type=F32)

    @pl.when(j == pl.num_programs(1) - 1)
    def _():
        ff = out_ref[...]
        ms = jnp.mean(ff * ff, axis=-1, keepdims=True)
        out_ref[...] = x_ref[...] + ff * lax.rsqrt(ms + EPS) * gpost_ref[...]


def _ffn(x2d, g_pre, w_in, w_out, g_post):
    m = x2d.shape[0]
    tm, tf = FFN_TM, FFN_TF
    nf = D_FF // tf
    return pl.pallas_call(
        _ffn_body,
        grid=(m // tm, nf),
        in_specs=[
            pl.BlockSpec((tm, D_MODEL), lambda i, j: (i, 0)),
            pl.BlockSpec((1, D_MODEL), lambda i, j: (0, 0)),
            pl.BlockSpec((D_MODEL, tf), lambda i, j: (0, j)),
            pl.BlockSpec((D_MODEL, tf), lambda i, j: (0, j + nf)),
            pl.BlockSpec((tf, D_MODEL), lambda i, j: (j, 0)),
            pl.BlockSpec((1, D_MODEL), lambda i, j: (0, 0)),
        ],
        out_specs=pl.BlockSpec((tm, D_MODEL), lambda i, j: (i, 0)),
        out_shape=jax.ShapeDtypeStruct((m, D_MODEL), F32),
        scratch_shapes=[pltpu.VMEM((tm, D_MODEL), BF16)],
        compiler_params=_cparams(("parallel", "arbitrary"), 60),
        name="ffn",
    )(x2d, g_pre, w_in, w_in, w_out, g_post)


def _prep_layer(w_in, conv_w, b_gates, g_head, w_a_out, w_b_out, b_merge, w_out,
                w_ffn_in, w_ffn_out):
    w_main = jnp.concatenate([w_in[:, :OFF_G], w_in[:, OFF_M:], w_in[:, OFF_B:OFF_M]], axis=1).astype(BF16)
    w_gate = jnp.pad(w_in[:, OFF_G:OFF_B], ((0, 0), (0, LANES - N_GATES))).astype(BF16)
    conv_w8 = jnp.pad(conv_w, ((0, SUBLANES - CONV_W), (0, 0)))
    bias_row = jnp.pad(b_gates[None, :], ((0, 0), (0, LANES - N_GATES)))
    return dict(
        w_main=w_main, w_gate=w_gate, conv_w8=conv_w8, bias_row=bias_row,
        g_head=g_head[None, :], wa=w_a_out.astype(BF16), wb=w_b_out.astype(BF16),
        b_merge=b_merge[None, :], wo=w_out.astype(BF16),
        w_ffn_in=w_ffn_in.astype(BF16), w_ffn_out=w_ffn_out.astype(BF16),
    )


def _layer(x, p, g_pre_mix, g_post_mix, g_pre_ffn, g_post_ffn):
    b, t, d = x.shape
    m = b * t
    x2d = x.reshape(m, d)
    z, zu, zg = _inproj(x2d, g_pre_mix[None, :], p["w_main"], p["w_gate"])
    z3 = z.reshape(b, t, Z_W)
    qc = _conv_silu(z3, p["conv_w8"], keys=False)
    ktc = _conv_silu(z3, p["conv_w8"], keys=True)
    gcol, grow = _gate_scans(zg.reshape(b, t, LANES), p["bias_row"])
    hf, hb = _mlstm(qc, ktc, z3, gcol, grow)
    fb = _fourier(zu.reshape(b, t, W_B))
    x1 = _outproj(hf.reshape(m, W_V), hb.reshape(m, W_V), z, p["g_head"], fb.reshape(m, W_B),
                  p["b_merge"], x2d, p["wa"], p["wb"], p["wo"], g_post_mix[None, :])
    y = _ffn(x1, g_pre_ffn[None, :], p["w_ffn_in"], p["w_ffn_out"], g_post_ffn[None, :])
    return y.reshape(b, t, d)


def kernel(x_prompt, x_sample, g_pre_mix, w_in, conv_w, b_gates, g_head, w_a_out, w_b_out, b_merge,
           w_out, g_post_mix, g_pre_ffn, w_ffn_in, w_ffn_out, g_post_ffn):
    depth = w_in.shape[0]
    layers = [
        _prep_layer(w_in[l], conv_w[l], b_gates[l], g_head[l], w_a_out[l], w_b_out[l], b_merge[l],
                    w_out[l], w_ffn_in[l], w_ffn_out[l])
        for l in range(depth)
    ]

    def trunk(x):
        for l in range(depth):
            x = _layer(x, layers[l], g_pre_mix[l], g_post_mix[l], g_pre_ffn[l], g_post_ffn[l])
        return x

    return (trunk(x_prompt), trunk(x_sample))
```

```python
import functools

import numpy as np
import jax
import jax.numpy as jnp
from jax import lax
from jax.experimental import pallas as pl
from jax.experimental.pallas import tpu as pltpu

F32 = jnp.float32
BF16 = jnp.bfloat16

D_MODEL = 2048
N_HEADS = 8
DK = 128
DV = 256
W_QK = N_HEADS * DK
W_V = N_HEADS * DV
N_GATES = 4 * N_HEADS
CONV_W = 5
CHUNK = 128
N_GROUPS_B = 8
GROUP_B = 128
W_B = N_GROUPS_B * GROUP_B
D_FF = 5632
OFF_V = 2 * W_QK
OFF_O = OFF_V + W_V
OFF_G = OFF_O + W_V
OFF_B = OFF_G + N_GATES
OFF_M = OFF_B + W_B
EPS = 1e-6

LANES = 128
SUBLANES = 8
MIB = 1024 * 1024


def _cparams(sem, vmem_mib):
    return pltpu.CompilerParams(dimension_semantics=sem, vmem_limit_bytes=vmem_mib * MIB)


IN_TM = 1024
IN_TN = 1024
ZC_QK = 0
ZC_V = ZC_QK + 2 * W_QK
ZC_O = ZC_V + W_V
ZC_MA = ZC_O + W_V
ZC_MB = ZC_MA + D_MODEL
ZC_U = ZC_MB + D_MODEL
Z_W = ZC_U + W_B
IN_NJ = Z_W // IN_TN
assert IN_TN == W_B and ZC_U == (IN_NJ - 1) * IN_TN


def _inproj_body(x_ref, g_ref, w_ref, wg_ref, z_ref, zu_ref, zg_ref, h_scr):
    j = pl.program_id(1)

    @pl.when(j == 0)
    def _():
        x = x_ref[...]
        ms = jnp.mean(x * x, axis=-1, keepdims=True)
        h = (x * lax.rsqrt(ms + EPS) * g_ref[...]).astype(BF16)
        h_scr[...] = h
        zg_ref[...] = jnp.dot(h, wg_ref[...], preferred_element_type=F32)

    acc = jnp.dot(h_scr[...], w_ref[...], preferred_element_type=F32)
    z_ref[...] = acc.astype(BF16)
    zu_ref[...] = acc


def _inproj(x2d, g, w_main, w_gate):
    m = x2d.shape[0]
    tm, tn = IN_TM, IN_TN
    out_shapes = (
        jax.ShapeDtypeStruct((m, Z_W), BF16),
        jax.ShapeDtypeStruct((m, W_B), F32),
        jax.ShapeDtypeStruct((m, LANES), F32),
    )
    return pl.pallas_call(
        _inproj_body,
        grid=(m // tm, IN_NJ),
        in_specs=[
            pl.BlockSpec((tm, D_MODEL), lambda i, j: (i, 0)),
            pl.BlockSpec((1, D_MODEL), lambda i, j: (0, 0)),
            pl.BlockSpec((D_MODEL, tn), lambda i, j: (0, j)),
            pl.BlockSpec((D_MODEL, LANES), lambda i, j: (0, 0)),
        ],
        out_specs=(
            pl.BlockSpec((tm, tn), lambda i, j: (i, j)),
            pl.BlockSpec((tm, W_B), lambda i, j: (i, 0)),
            pl.BlockSpec((tm, LANES), lambda i, j: (i, 0)),
        ),
        out_shape=out_shapes,
        scratch_shapes=[pltpu.VMEM((tm, D_MODEL), BF16)],
        compiler_params=_cparams(("parallel", "arbitrary"), 56),
        name="inproj",
    )(x2d, g, w_main, w_gate)


CONV_TQ = 512
CONV_CW = 512
CONV_HALO = 16
CONV_NQ = W_QK // CONV_CW


def _conv_body(prev_ref, cur_ref, next_ref, w_ref, out_ref, *, keys):
    i = pl.program_id(1)
    n = pl.num_programs(1)
    tq = cur_ref.shape[0]
    hl = CONV_HALO
    pv = jnp.where(i > 0, prev_ref[...].astype(F32), 0.0)
    nx = jnp.where(i < n - 1, next_ref[...].astype(F32), 0.0)
    ext = jnp.concatenate([pv, cur_ref[...].astype(F32), nx], axis=0)
    rows = tq + 2 * hl
    acc = None
    for j in range(CONV_W):
        d = j - CONV_W // 2
        sh = ext if d == 0 else pltpu.roll(ext, (-d) % rows, 0)
        term = sh[hl:hl + tq] * w_ref[j:j + 1, :]
        acc = term if acc is None else acc + term
    y = acc * jax.nn.sigmoid(acc)
    if keys:
        yt = (y * (DK ** -0.5)).T.astype(BF16)
        for c in range(tq // CHUNK):
            out_ref[c] = yt[:, c * CHUNK:(c + 1) * CHUNK]
    else:
        out_ref[...] = y.astype(BF16)


def _conv_silu(z3, conv_w8, keys):
    b, t, _ = z3.shape
    tq, cw, hl = CONV_TQ, CONV_CW, CONV_HALO
    r = tq // hl
    nh = t // hl
    c0 = CONV_NQ if keys else 0
    if keys:
        out_spec = pl.BlockSpec((None, tq // CHUNK, cw, CHUNK), lambda bb, i, k: (bb, i, k, 0))
        out_shape = jax.ShapeDtypeStruct((b, t // CHUNK, W_QK, CHUNK), BF16)
    else:
        out_spec = pl.BlockSpec((None, tq, cw), lambda bb, i, k: (bb, i, k))
        out_shape = jax.ShapeDtypeStruct((b, t, W_QK), BF16)
    return pl.pallas_call(
        functools.partial(_conv_body, keys=keys),
        grid=(b, t // tq, CONV_NQ),
        in_specs=[
            pl.BlockSpec((None, hl, cw), lambda bb, i, k: (bb, jnp.maximum(i * r - 1, 0), c0 + k)),
            pl.BlockSpec((None, tq, cw), lambda bb, i, k: (bb, i, c0 + k)),
            pl.BlockSpec((None, hl, cw), lambda bb, i, k: (bb, jnp.minimum((i + 1) * r, nh - 1), c0 + k)),
            pl.BlockSpec((SUBLANES, cw), lambda bb, i, k: (0, c0 + k)),
        ],
        out_specs=out_spec,
        out_shape=out_shape,
        compiler_params=_cparams(("parallel", "parallel", "parallel"), 40),
        name="conv_silu_k" if keys else "conv_silu_q",
    )(z3, z3, z3, conv_w8)


GATE_TG = 1024


def _log_sigmoid(x):
    return jnp.minimum(x, 0.0) - jnp.log1p(jnp.exp(-jnp.abs(x)))


def _gates_body(zg_ref, bias_ref, col_ref, row_ref):
    g = zg_ref[...] + bias_ref[...]
    gt = g.T
    h = N_HEADS
    li_f, lf_f = gt[0:h], _log_sigmoid(gt[h:2 * h])
    li_b, lf_b = gt[2 * h:3 * h], _log_sigmoid(gt[3 * h:4 * h])
    tg = gt.shape[1]
    pos = lax.broadcasted_iota(jnp.int32, (h, tg), 1) & (CHUNK - 1)

    def scan(x, op, fill, reverse):
        s = 1
        while s < CHUNK:
            if reverse:
                sh = pltpu.roll(x, tg - s, 1)
                ok = pos < CHUNK - s
            else:
                sh = pltpu.roll(x, s, 1)
                ok = pos >= s
            x = op(x, jnp.where(ok, sh, fill))
            s *= 2
        return x

    b_f = scan(lf_f, jnp.add, 0.0, False)
    b_b = scan(lf_b, jnp.add, 0.0, True)
    w_f = li_f - b_f
    w_b = li_b - b_b
    a_f = scan(w_f, jnp.maximum, -jnp.inf, False)
    a_b = scan(w_b, jnp.maximum, -jnp.inf, True)
    row_ref[...] = jnp.concatenate([w_f, w_b], axis=0)
    pad = jnp.zeros((LANES - 4 * h, tg), F32)
    col_ref[...] = jnp.concatenate([b_f, a_f, b_b, a_b, pad], axis=0).T


def _gate_scans(zg3, bias_row):
    b, t, _ = zg3.shape
    tg = GATE_TG
    return pl.pallas_call(
        _gates_body,
        grid=(b, t // tg),
        in_specs=[
            pl.BlockSpec((None, tg, LANES), lambda bb, i: (bb, i, 0)),
            pl.BlockSpec((1, LANES), lambda bb, i: (0, 0)),
        ],
        out_specs=(
            pl.BlockSpec((None, tg, LANES), lambda bb, i: (bb, i, 0)),
            pl.BlockSpec((None, 2 * N_HEADS, tg), lambda bb, i: (bb, 0, i)),
        ),
        out_shape=(
            jax.ShapeDtypeStruct((b, t, LANES), F32),
            jax.ShapeDtypeStruct((b, 2 * N_HEADS, t), F32),
        ),
        compiler_params=_cparams(("parallel", "parallel"), 32),
        name="gate_scans",
    )(zg3, bias_row)


ST_W = DV + LANES
MLSTM_CPS = 2


def _mlstm_dir(q_ref, kt_ref, v_ref, col_ref, row_ref, h_ref, st_ref, m_ref, d, reverse, sub):
    L = CHUNK
    r0 = sub * L
    t_idx = lax.broadcasted_iota(jnp.int32, (L, L), 0)
    s_idx = lax.broadcasted_iota(jnp.int32, (L, L), 1)
    mask = (s_idx >= t_idx) if reverse else (s_idx <= t_idx)
    last = 0 if reverse else L - 1
    ones = jnp.ones((L, LANES), BF16)
    for hd in range(N_HEADS):
        q = q_ref[r0:r0 + L, hd * DK:(hd + 1) * DK]
        kt = kt_ref[sub, hd * DK:(hd + 1) * DK, :]
        v = v_ref[r0:r0 + L, hd * DV:(hd + 1) * DV]
        cb = 2 * N_HEADS * d + hd
        bcol = col_ref[r0:r0 + L, cb:cb + 1]
        acol = col_ref[r0:r0 + L, cb + N_HEADS:cb + N_HEADS + 1]
        wrow = row_ref[N_HEADS * d + hd:N_HEADS * d + hd + 1, r0:r0 + L]
        m11 = m_ref[d, hd:hd + 1, 0:1]
        st = st_ref[d, hd]

        mx = jnp.maximum(acol, m11)
        mx_last = mx[last:last + 1, :]
        b_last = bcol[last:last + 1, :]

        s_qk = jnp.dot(q, kt, preferred_element_type=F32)
        mxb = jnp.broadcast_to(mx, (L, LANES))
        p = jnp.exp(wrow - mxb)
        s_mat = jnp.where(mask, s_qk * p, 0.0).astype(BF16)
        qw = (q.astype(F32) * jnp.exp(m11 - mxb)).astype(BF16)
        lhs = jnp.concatenate([s_mat, qw], axis=1)
        vext = jnp.concatenate([v, ones], axis=1)
        rhs = jnp.concatenate([vext, st.astype(BF16)], axis=0)
        res = jnp.dot(lhs, rhs, preferred_element_type=F32)
        nq = res[:, DV:]
        den = jnp.maximum(jnp.abs(nq), jnp.exp(-(bcol + mx)))
        inv = 1.0 / den
        hout = res[:, :DV] * jnp.concatenate([inv, inv], axis=1)
        h_ref[r0:r0 + L, hd * DV:(hd + 1) * DV] = hout.astype(h_ref.dtype)

        w_s = jnp.exp(wrow - mx_last)
        kwt = (kt.astype(F32) * w_s).astype(BF16)
        upd = jnp.dot(kwt, vext, preferred_element_type=F32)
        st_ref[d, hd] = jnp.exp(m11 - mx_last) * st + upd
        m_ref[d, hd:hd + 1, :] = jnp.broadcast_to(b_last + mx_last, (1, LANES))


def _mlstm_body(qf_ref, qb_ref, ktf_ref, ktb_ref, vf_ref, vb_ref, colf_ref, colb_ref, rowf_ref, rowb_ref,
                hf_ref, hb_ref, st_ref, m_ref):
    @pl.when(pl.program_id(1) == 0)
    def _():
        st_ref[...] = jnp.zeros(st_ref.shape, F32)
        m_ref[...] = jnp.zeros(m_ref.shape, F32)

    for s in range(MLSTM_CPS):
        _mlstm_dir(qf_ref, ktf_ref, vf_ref, colf_ref, rowf_ref, hf_ref, st_ref, m_ref, 0, False, s)
        _mlstm_dir(qb_ref, ktb_ref, vb_ref, colb_ref, rowb_ref, hb_ref, st_ref, m_ref, 1, True,
                   MLSTM_CPS - 1 - s)


def _mlstm(qc, ktc, z3, gcol, grow):
    b, t, _ = qc.shape
    cps = MLSTM_CPS
    L = CHUNK * cps
    ns = t // L
    vcol = ZC_V // W_V
    fwd3 = lambda bb, i: (bb, i, 0)
    bwd3 = lambda bb, i: (bb, ns - 1 - i, 0)
    return pl.pallas_call(
        _mlstm_body,
        grid=(b, ns),
        in_specs=[
            pl.BlockSpec((None, L, W_QK), fwd3),
            pl.BlockSpec((None, L, W_QK), bwd3),
            pl.BlockSpec((None, cps, W_QK, CHUNK), lambda bb, i: (bb, i, 0, 0)),
            pl.BlockSpec((None, cps, W_QK, CHUNK), lambda bb, i: (bb, ns - 1 - i, 0, 0)),
            pl.BlockSpec((None, L, W_V), lambda bb, i: (bb, i, vcol)),
            pl.BlockSpec((None, L, W_V), lambda bb, i: (bb, ns - 1 - i, vcol)),
            pl.BlockSpec((None, L, LANES), fwd3),
            pl.BlockSpec((None, L, LANES), bwd3),
            pl.BlockSpec((None, 2 * N_HEADS, L), lambda bb, i: (bb, 0, i)),
            pl.BlockSpec((None, 2 * N_HEADS, L), lambda bb, i: (bb, 0, ns - 1 - i)),
        ],
        out_specs=(
            pl.BlockSpec((None, L, W_V), fwd3),
            pl.BlockSpec((None, L, W_V), bwd3),
        ),
        out_shape=(
            jax.ShapeDtypeStruct((b, t, W_V), BF16),
            jax.ShapeDtypeStruct((b, t, W_V), BF16),
        ),
        scratch_shapes=[
            pltpu.VMEM((2, N_HEADS, DK, ST_W), F32),
            pltpu.VMEM((2, N_HEADS, LANES), F32),
        ],
        compiler_params=_cparams(("parallel", "arbitrary"), 40),
        name="mlstm",
    )(qc, qc, ktc, ktc, z3, z3, gcol, gcol, grow, grow)


FFT_N2 = 128
FFT_R = SUBLANES
FFT1_CB = W_B
FFT2_CB = 512


@functools.lru_cache(maxsize=None)
def _fft_tables(t):
    n2 = FFT_N2
    n1 = t // n2
    k1 = np.arange(n1)[:, None].astype(np.float64)
    t1 = np.arange(n1)[None, :].astype(np.float64)
    t2 = np.arange(n2)[:, None, None].astype(np.float64)
    ang = 2.0 * np.pi * (k1 * t1 / n1)[None] + 2.0 * np.pi * (k1[None] * t2 / t)
    tab1 = np.concatenate([np.cos(ang), np.sin(ang)], axis=1) / np.sqrt(n1)
    c = np.arange(GROUP_B)[:, None].astype(np.float64)
    cc = np.arange(GROUP_B)[None, :].astype(np.float64)
    angc = 2.0 * np.pi * c * cc / GROUP_B
    chan = np.concatenate([np.cos(angc), np.sin(angc)], axis=1) / np.sqrt(GROUP_B)
    k2 = np.arange(n2)[:, None].astype(np.float64)
    tt = np.arange(n2)[None, :].astype(np.float64)
    ang2 = 2.0 * np.pi * k2 * tt / n2
    tab2 = np.concatenate([np.cos(ang2), -np.sin(ang2)], axis=1) / np.sqrt(n2)
    return tuple(np.asarray(a.astype(np.float32)).astype(BF16) for a in (tab1, chan, tab2))


def _pack_complex(re, im):
    hi = lax.bitcast_convert_type(re.astype(BF16).astype(F32), jnp.uint32)
    lo = lax.bitcast_convert_type(im.astype(BF16).astype(F32), jnp.uint32)
    return hi | (lo >> 16)


def _unpack_complex(w):
    re = lax.bitcast_convert_type(w & jnp.uint32(0xFFFF0000), F32)
    im = lax.bitcast_convert_type(w << 16, F32)
    return re.astype(BF16), im.astype(BF16)


def _fft1_body(u_ref, tab_ref, e_ref, a_ref, sin_scr, uri_scr, sa_scr):
    n1 = u_ref.shape[0]
    ng = u_ref.shape[2] // LANES
    for g in range(ng):
        sin_scr[g] = u_ref[:, :, g * LANES:(g + 1) * LANES].reshape(n1 * FFT_R, LANES)
    for r in range(FFT_R):
        x = jnp.concatenate(
            [sin_scr.at[g][pl.ds(r, n1, stride=FFT_R), :] for g in range(ng)], axis=1)
        uri = jnp.dot(tab_ref[r], x.astype(BF16), preferred_element_type=F32)
        uri_scr[r] = uri.astype(BF16)
    e = e_ref[...]
    for g in range(ng):
        lhs = uri_scr[:, :, g * LANES:(g + 1) * LANES].reshape(FFT_R * 2 * n1, LANES)
        rr = jnp.dot(lhs, e, preferred_element_type=F32).reshape(FFT_R, 2 * n1, 2 * LANES)
        for r in range(FFT_R):
            ar = rr[r, :n1, :LANES] - rr[r, n1:, LANES:]
            ai = rr[r, :n1, LANES:] + rr[r, n1:, :LANES]
            sa_scr.at[g][pl.ds(r, n1, stride=FFT_R), :] = _pack_complex(ar, ai)
    for g in range(ng):
        a_ref[:, :, g * LANES:(g + 1) * LANES] = sa_scr[g].reshape(n1, FFT_R, LANES)


def _fft2_body(a_ref, tab_ref, y_ref, s_scr):
    n2 = a_ref.shape[1]
    ng = a_ref.shape[2] // LANES
    tab = tab_ref[...]
    for r in range(FFT_R):
        br, bi = _unpack_complex(a_ref[r])
        bri = jnp.concatenate([br, bi], axis=0)
        y = jnp.dot(tab, bri, preferred_element_type=F32)
        for g in range(ng):
            s_scr.at[g][pl.ds(r, n2, stride=FFT_R), :] = y[:, g * LANES:(g + 1) * LANES]
    for g in range(ng):
        y_ref[:, :, g * LANES:(g + 1) * LANES] = s_scr[g].reshape(n2, FFT_R, LANES)


def _fourier(zu3):
    b, t, c = zu3.shape
    n2 = FFT_N2
    n1 = t // n2
    tab1, chan, tab2 = _fft_tables(t)
    tab1, chan, tab2 = jnp.asarray(tab1), jnp.asarray(chan), jnp.asarray(tab2)
    u4 = zu3.reshape(b, n1, n2, c)
    cb1, cb2, r = FFT1_CB, FFT2_CB, FFT_R
    a = pl.pallas_call(
        _fft1_body,
        grid=(b, n2 // r, c // cb1),
        in_specs=[
            pl.BlockSpec((None, n1, r, cb1), lambda bb, j, k: (bb, 0, j, k)),
            pl.BlockSpec((r, 2 * n1, n1), lambda bb, j, k: (j, 0, 0)),
            pl.BlockSpec((GROUP_B, 2 * GROUP_B), lambda bb, j, k: (0, 0)),
        ],
        out_specs=pl.BlockSpec((None, n1, r, cb1), lambda bb, j, k: (bb, 0, j, k)),
        out_shape=jax.ShapeDtypeStruct((b, n1, n2, c), jnp.uint32),
        scratch_shapes=[
            pltpu.VMEM((cb1 // LANES, n1 * r, LANES), F32),
            pltpu.VMEM((r, 2 * n1, cb1), BF16),
            pltpu.VMEM((cb1 // LANES, n1 * r, LANES), jnp.uint32),
        ],
        compiler_params=_cparams(("parallel", "parallel", "parallel"), 48),
        name="fft_stage1",
    )(u4, tab1, chan)
    y = pl.pallas_call(
        _fft2_body,
        grid=(b, n1 // r, c // cb2),
        in_specs=[
            pl.BlockSpec((None, r, n2, cb2), lambda bb, j, k: (bb, j, 0, k)),
            pl.BlockSpec((n2, 2 * n2), lambda bb, j, k: (0, 0)),
        ],
        out_specs=pl.BlockSpec((None, n2, r, cb2), lambda bb, j, k: (bb, 0, j, k)),
        out_shape=jax.ShapeDtypeStruct((b, n2, n1, c), F32),
        scratch_shapes=[pltpu.VMEM((cb2 // LANES, n2 * r, LANES), F32)],
        compiler_params=_cparams(("parallel", "parallel", "parallel"), 40),
        name="fft_stage2",
    )(a, tab2)
    return y.reshape(b, t, c)


OUT_TM = 256


def _rowsum_lanes(x, ones_bf16):
    return jnp.dot(x.astype(BF16), ones_bf16, preferred_element_type=F32)


def _outproj_body(hf_ref, hb_ref, o_ref, gh_ref, fb_ref, ma_ref, mb_ref, bma_ref, bmb_ref, x_ref,
                  wa_ref, wb_ref, wo_ref, gpost_ref, out_ref):
    ones = jnp.ones((DV, LANES), BF16)
    ha = hf_ref[...].astype(F32) + hb_ref[...].astype(F32)
    parts = []
    for hd in range(N_HEADS):
        hh = ha[:, hd * DV:(hd + 1) * DV]
        ssum = _rowsum_lanes(hh * hh, ones)
        inv = lax.rsqrt(ssum * (1.0 / DV) + EPS)
        parts.append(hh * jnp.concatenate([inv, inv], axis=1))
    han = jnp.concatenate(parts, axis=1) * gh_ref[...]
    han = han * jax.nn.sigmoid(o_ref[...].astype(F32))
    ya = jnp.dot(han.astype(BF16), wa_ref[...], preferred_element_type=F32)
    yb = jnp.dot(fb_ref[...].astype(BF16), wb_ref[...], preferred_element_type=F32)
    ga = jax.nn.sigmoid(ma_ref[...].astype(F32) + bma_ref[...])
    gb = jax.nn.sigmoid(mb_ref[...].astype(F32) + bmb_ref[...])
    mixin = (ga * ya + gb * yb).astype(BF16)
    mix = jnp.dot(mixin, wo_ref[...], preferred_element_type=F32)
    ms = jnp.mean(mix * mix, axis=-1, keepdims=True)
    out_ref[...] = x_ref[...] + mix * lax.rsqrt(ms + EPS) * gpost_ref[...]


def _outproj(hf, hb, z, g_head, fb, b_merge, x2d, wa, wb, wo, g_post):
    m = x2d.shape[0]
    tm = OUT_TM
    row = lambda i: (i, 0)
    const = lambda i: (0, 0)
    resident = functools.partial(pl.BlockSpec, index_map=const, pipeline_mode=pl.Buffered(1))
    return pl.pallas_call(
        _outproj_body,
        grid=(m // tm,),
        in_specs=[
            pl.BlockSpec((tm, W_V), row),
            pl.BlockSpec((tm, W_V), row),
            pl.BlockSpec((tm, W_V), lambda i: (i, ZC_O // W_V)),
            pl.BlockSpec((1, W_V), const),
            pl.BlockSpec((tm, W_B), row),
            pl.BlockSpec((tm, D_MODEL), lambda i: (i, ZC_MA // D_MODEL)),
            pl.BlockSpec((tm, D_MODEL), lambda i: (i, ZC_MB // D_MODEL)),
            pl.BlockSpec((1, D_MODEL), lambda i: (0, 0)),
            pl.BlockSpec((1, D_MODEL), lambda i: (0, 1)),
            pl.BlockSpec((tm, D_MODEL), row),
            resident((W_V, D_MODEL)),
            resident((W_B, D_MODEL)),
            resident((D_MODEL, D_MODEL)),
            pl.BlockSpec((1, D_MODEL), const),
        ],
        out_specs=pl.BlockSpec((tm, D_MODEL), row),
        out_shape=jax.ShapeDtypeStruct((m, D_MODEL), F32),
        compiler_params=_cparams(("parallel",), 56),
        name="outproj",
    )(hf, hb, z, g_head, fb, z, z, b_merge, b_merge, x2d, wa, wb, wo, g_post)


FFN_TM = 1024
FFN_TF = 256


def _ffn_body(x_ref, gpre_ref, wg_ref, wu_ref, wo_ref, gpost_ref, out_ref, h_scr):
    j = pl.program_id(1)

    @pl.when(j == 0)
    def _():
        x = x_ref[...]
        ms = jnp.mean(x * x, axis=-1, keepdims=True)
        h_scr[...] = (x * lax.rsqrt(ms + EPS) * gpre_ref[...]).astype(BF16)
        out_ref[...] = jnp.zeros(out_ref.shape, F32)

    h = h_scr[...]
    g = jnp.dot(h, wg_ref[...], preferred_element_type=F32)
    u = jnp.dot(h, wu_ref[...], preferred_element_type=F32)
    act = (g * jax.nn.sigmoid(g) * u).astype(BF16)
    out_ref[...] += jnp.dot(act, wo_ref[...], preferred_element_type=F32)

    @pl.when(j == pl.num_programs(1) - 1)
    def _():
        ff = out_ref[...]
        ms = jnp.mean(ff * ff, axis=-1, keepdims=True)
        out_ref[...] = x_ref[...] + ff * lax.rsqrt(ms + EPS) * gpost_ref[...]


def _ffn(x2d, g_pre, w_in, w_out, g_post):
    m = x2d.shape[0]
    tm, tf = FFN_TM, FFN_TF
    nf = D_FF // tf
    return pl.pallas_call(
        _ffn_body,
        grid=(m // tm, nf),
        in_specs=[
            pl.BlockSpec((tm, D_MODEL), lambda i, j: (i, 0)),
            pl.BlockSpec((1, D_MODEL), lambda i, j: (0, 0)),
            pl.BlockSpec((D_MODEL, tf), lambda i, j: (0, j)),
            pl.BlockSpec((D_MODEL, tf), lambda i, j: (0, j + nf)),
            pl.BlockSpec((tf, D_MODEL), lambda i, j: (j, 0)),
            pl.BlockSpec((1, D_MODEL), lambda i, j: (0, 0)),
        ],
        out_specs=pl.BlockSpec((tm, D_MODEL), lambda i, j: (i, 0)),
        out_shape=jax.ShapeDtypeStruct((m, D_MODEL), F32),
        scratch_shapes=[pltpu.VMEM((tm, D_MODEL), BF16)],
        compiler_params=_cparams(("parallel", "arbitrary"), 60),
        name="ffn",
    )(x2d, g_pre, w_in, w_in, w_out, g_post)


def _prep_layer(w_in, conv_w, b_gates, g_head, w_a_out, w_b_out, b_merge, w_out,
                w_ffn_in, w_ffn_out):
    w_main = jnp.concatenate([w_in[:, :OFF_G], w_in[:, OFF_M:], w_in[:, OFF_B:OFF_M]], axis=1).astype(BF16)
    w_gate = jnp.pad(w_in[:, OFF_G:OFF_B], ((0, 0), (0, LANES - N_GATES))).astype(BF16)
    conv_w8 = jnp.pad(conv_w, ((0, SUBLANES - CONV_W), (0, 0)))
    bias_row = jnp.pad(b_gates[None, :], ((0, 0), (0, LANES - N_GATES)))
    return dict(
        w_main=w_main, w_gate=w_gate, conv_w8=conv_w8, bias_row=bias_row,
        g_head=g_head[None, :], wa=w_a_out.astype(BF16), wb=w_b_out.astype(BF16),
        b_merge=b_merge[None, :], wo=w_out.astype(BF16),
        w_ffn_in=w_ffn_in.astype(BF16), w_ffn_out=w_ffn_out.astype(BF16),
    )


def _layer(x, p, g_pre_mix, g_post_mix, g_pre_ffn, g_post_ffn):
    b, t, d = x.shape
    m = b * t
    x2d = x.reshape(m, d)
    z, zu, zg = _inproj(x2d, g_pre_mix[None, :], p["w_main"], p["w_gate"])
    z3 = z.reshape(b, t, Z_W)
    qc = _conv_silu(z3, p["conv_w8"], keys=False)
    ktc = _conv_silu(z3, p["conv_w8"], keys=True)
    gcol, grow = _gate_scans(zg.reshape(b, t, LANES), p["bias_row"])
    hf, hb = _mlstm(qc, ktc, z3, gcol, grow)
    fb = _fourier(zu.reshape(b, t, W_B))
    x1 = _outproj(hf.reshape(m, W_V), hb.reshape(m, W_V), z, p["g_head"], fb.reshape(m, W_B),
                  p["b_merge"], x2d, p["wa"], p["wb"], p["wo"], g_post_mix[None, :])
    y = _ffn(x1, g_pre_ffn[None, :], p["w_ffn_in"], p["w_ffn_out"], g_post_ffn[None, :])
    return y.reshape(b, t, d)


def kernel(x_prompt, x_sample, g_pre_mix, w_in, conv_w, b_gates, g_head, w_a_out, w_b_out, b_merge,
           w_out, g_post_mix, g_pre_ffn, w_ffn_in, w_ffn_out, g_post_ffn):
    depth = w_in.shape[0]
    layers = [
        _prep_layer(w_in[l], conv_w[l], b_gates[l], g_head[l], w_a_out[l], w_b_out[l], b_merge[l],
                    w_out[l], w_ffn_in[l], w_ffn_out[l])
        for l in range(depth)
    ]

    def trunk(x):
        for l in range(depth):
            x = _layer(x, layers[l], g_pre_mix[l], g_post_mix[l], g_pre_ffn[l], g_post_ffn[l])
        return x

    return (trunk(x_prompt), trunk(x_sample))
```

```python
import functools

import numpy as np
import jax
import jax.numpy as jnp
from jax import lax
from jax.experimental import pallas as pl
from jax.experimental.pallas import tpu as pltpu

F32 = jnp.float32
BF16 = jnp.bfloat16

D_MODEL = 2048
N_HEADS = 8
DK = 128
DV = 256
W_QK = N_HEADS * DK
W_V = N_HEADS * DV
N_GATES = 4 * N_HEADS
CONV_W = 5
CHUNK = 128
N_GROUPS_B = 8
GROUP_B = 128
W_B = N_GROUPS_B * GROUP_B
D_FF = 5632
OFF_V = 2 * W_QK
OFF_O = OFF_V + W_V
OFF_G = OFF_O + W_V
OFF_B = OFF_G + N_GATES
OFF_M = OFF_B + W_B
EPS = 1e-6

LANES = 128
SUBLANES = 8
MIB = 1024 * 1024


def _cparams(sem, vmem_mib):
    return pltpu.CompilerParams(dimension_semantics=sem, vmem_limit_bytes=vmem_mib * MIB)


IN_TM = 1024
IN_TN = 1024
ZC_QK = 0
ZC_V = ZC_QK + 2 * W_QK
ZC_O = ZC_V + W_V
ZC_MA = ZC_O + W_V
ZC_MB = ZC_MA + D_MODEL
ZC_U = ZC_MB + D_MODEL
Z_W = ZC_U + W_B
IN_NJ = Z_W // IN_TN
assert IN_TN == W_B and ZC_U == (IN_NJ - 1) * IN_TN


def _inproj_body(x_ref, g_ref, w_ref, wg_ref, z_ref, zu_ref, zg_ref, h_scr):
    j = pl.program_id(1)

    @pl.when(j == 0)
    def _():
        x = x_ref[...]
        ms = jnp.mean(x * x, axis=-1, keepdims=True)
        h = (x * lax.rsqrt(ms + EPS) * g_ref[...]).astype(BF16)
        h_scr[...] = h
        zg_ref[...] = jnp.dot(h, wg_ref[...], preferred_element_type=F32)

    acc = jnp.dot(h_scr[...], w_ref[...], preferred_element_type=F32)
    z_ref[...] = acc.astype(BF16)
    zu_ref[...] = acc


def _inproj(x2d, g, w_main, w_gate):
    m = x2d.shape[0]
    tm, tn = IN_TM, IN_TN
    out_shapes = (
        jax.ShapeDtypeStruct((m, Z_W), BF16),
        jax.ShapeDtypeStruct((m, W_B), F32),
        jax.ShapeDtypeStruct((m, LANES), F32),
    )
    return pl.pallas_call(
        _inproj_body,
        grid=(m // tm, IN_NJ),
        in_specs=[
            pl.BlockSpec((tm, D_MODEL), lambda i, j: (i, 0)),
            pl.BlockSpec((1, D_MODEL), lambda i, j: (0, 0)),
            pl.BlockSpec((D_MODEL, tn), lambda i, j: (0, j)),
            pl.BlockSpec((D_MODEL, LANES), lambda i, j: (0, 0)),
        ],
        out_specs=(
            pl.BlockSpec((tm, tn), lambda i, j: (i, j)),
            pl.BlockSpec((tm, W_B), lambda i, j: (i, 0)),
            pl.BlockSpec((tm, LANES), lambda i, j: (i, 0)),
        ),
        out_shape=out_shapes,
        scratch_shapes=[pltpu.VMEM((tm, D_MODEL), BF16)],
        compiler_params=_cparams(("parallel", "arbitrary"), 56),
        name="inproj",
    )(x2d, g, w_main, w_gate)


CONV_TQ = 512
CONV_CW = 512
CONV_HALO = 16
CONV_NQ = W_QK // CONV_CW


def _conv_body(prev_ref, cur_ref, next_ref, w_ref, out_ref, *, keys):
    i = pl.program_id(1)
    n = pl.num_programs(1)
    tq = cur_ref.shape[0]
    hl = CONV_HALO
    pv = jnp.where(i > 0, prev_ref[...].astype(F32), 0.0)
    nx = jnp.where(i < n - 1, next_ref[...].astype(F32), 0.0)
    ext = jnp.concatenate([pv, cur_ref[...].astype(F32), nx], axis=0)
    rows = tq + 2 * hl
    acc = None
    for j in range(CONV_W):
        d = j - CONV_W // 2
        sh = ext if d == 0 else pltpu.roll(ext, (-d) % rows, 0)
        term = sh[hl:hl + tq] * w_ref[j:j + 1, :]
        acc = term if acc is None else acc + term
    y = acc * jax.nn.sigmoid(acc)
    if keys:
        yt = (y * (DK ** -0.5)).T.astype(BF16)
        for c in range(tq // CHUNK):
            out_ref[c] = yt[:, c * CHUNK:(c + 1) * CHUNK]
    else:
        out_ref[...] = y.astype(BF16)


def _conv_silu(z3, conv_w8, keys):
    b, t, _ = z3.shape
    tq, cw, hl = CONV_TQ, CONV_CW, CONV_HALO
    r = tq // hl
    nh = t // hl
    c0 = CONV_NQ if keys else 0
    if keys:
        out_spec = pl.BlockSpec((None, tq // CHUNK, cw, CHUNK), lambda bb, i, k: (bb, i, k, 0))
        out_shape = jax.ShapeDtypeStruct((b, t // CHUNK, W_QK, CHUNK), BF16)
    else:
        out_spec = pl.BlockSpec((None, tq, cw), lambda bb, i, k: (bb, i, k))
        out_shape = jax.ShapeDtypeStruct((b, t, W_QK), BF16)
    return pl.pallas_call(
        functools.partial(_conv_body, keys=keys),
        grid=(b, t // tq, CONV_NQ),
        in_specs=[
            pl.BlockSpec((None, hl, cw), lambda bb, i, k: (bb, jnp.maximum(i * r - 1, 0), c0 + k)),
            pl.BlockSpec((None, tq, cw), lambda bb, i, k: (bb, i, c0 + k)),
            pl.BlockSpec((None, hl, cw), lambda bb, i, k: (bb, jnp.minimum((i + 1) * r, nh - 1), c0 + k)),
            pl.BlockSpec((SUBLANES, cw), lambda bb, i, k: (0, c0 + k)),
        ],
        out_specs=out_spec,
        out_shape=out_shape,
        compiler_params=_cparams(("parallel", "parallel", "parallel"), 40),
        name="conv_silu_k" if keys else "conv_silu_q",
    )(z3, z3, z3, conv_w8)


GATE_TG = 1024


def _log_sigmoid(x):
    return jnp.minimum(x, 0.0) - jnp.log1p(jnp.exp(-jnp.abs(x)))


def _gates_body(zg_ref, bias_ref, col_ref, row_ref):
    g = zg_ref[...] + bias_ref[...]
    gt = g.T
    h = N_HEADS
    li_f, lf_f = gt[0:h], _log_sigmoid(gt[h:2 * h])
    li_b, lf_b = gt[2 * h:3 * h], _log_sigmoid(gt[3 * h:4 * h])
    tg = gt.shape[1]
    pos = lax.broadcasted_iota(jnp.int32, (h, tg), 1) & (CHUNK - 1)

    def scan(x, op, fill, reverse):
        s = 1
        while s < CHUNK:
            if reverse:
                sh = pltpu.roll(x, tg - s, 1)
                ok = pos < CHUNK - s
            else:
                sh = pltpu.roll(x, s, 1)
                ok = pos >= s
            x = op(x, jnp.where(ok, sh, fill))
            s *= 2
        return x

    b_f = scan(lf_f, jnp.add, 0.0, False)
    b_b = scan(lf_b, jnp.add, 0.0, True)
    w_f = li_f - b_f
    w_b = li_b - b_b
    a_f = scan(w_f, jnp.maximum, -jnp.inf, False)
    a_b = scan(w_b, jnp.maximum, -jnp.inf, True)
    row_ref[...] = jnp.concatenate([w_f, w_b], axis=0)
    pad = jnp.zeros((LANES - 4 * h, tg), F32)
    col_ref[...] = jnp.concatenate([b_f, a_f, b_b, a_b, pad], axis=0).T


def _gate_scans(zg3, bias_row):
    b, t, _ = zg3.shape
    tg = GATE_TG
    return pl.pallas_call(
        _gates_body,
        grid=(b, t // tg),
        in_specs=[
            pl.BlockSpec((None, tg, LANES), lambda bb, i: (bb, i, 0)),
            pl.BlockSpec((1, LANES), lambda bb, i: (0, 0)),
        ],
        out_specs=(
            pl.BlockSpec((None, tg, LANES), lambda bb, i: (bb, i, 0)),
            pl.BlockSpec((None, 2 * N_HEADS, tg), lambda bb, i: (bb, 0, i)),
        ),
        out_shape=(
            jax.ShapeDtypeStruct((b, t, LANES), F32),
            jax.ShapeDtypeStruct((b, 2 * N_HEADS, t), F32),
        ),
        compiler_params=_cparams(("parallel", "parallel"), 32),
        name="gate_scans",
    )(zg3, bias_row)


ST_W = DV + LANES
MLSTM_NCH = 2 * N_HEADS


def _chain_refs(c, refs):
    d, hd = divmod(c, N_HEADS)
    return d, hd, d == 1, refs[d]


def _mlstm_gates(c, refs, m_ref, p_scr, lhs_scr, etb_scr, kwt_scr, wc_scr):
    L = CHUNK
    d, hd, reverse, (q_ref, kt_ref, v_ref, col_ref, row_ref, h_ref) = _chain_refs(c, refs)
    last = 0 if reverse else L - 1
    cb = 2 * N_HEADS * d + hd
    colblk = col_ref[...]
    lane_b = jnp.full((L, LANES), cb, jnp.int32)
    b_b = jnp.take_along_axis(colblk, lane_b, axis=1)
    a_b = jnp.take_along_axis(colblk, lane_b + N_HEADS, axis=1)
    wrow = row_ref[N_HEADS * d + hd:N_HEADS * d + hd + 1, :]
    m11 = m_ref[d, hd:hd + 1, 0:1]
    mxb = jnp.maximum(a_b, m11)
    mx_last = mxb[last:last + 1, :]
    b_last = b_b[last:last + 1, :]
    p_scr[c] = jnp.exp(wrow - mxb)
    q = q_ref[:, hd * DK:(hd + 1) * DK]
    lhs_scr[c, :, L:] = (q.astype(F32) * jnp.exp(m11 - mxb)).astype(BF16)
    etb_scr[c] = jnp.exp(-(b_b + mxb))
    kt = kt_ref[0, hd * DK:(hd + 1) * DK, :]
    kwt_scr[c] = (kt.astype(F32) * jnp.exp(wrow - mx_last)).astype(BF16)
    wc_scr[c] = jnp.broadcast_to(jnp.exp(m11 - mx_last), (SUBLANES, LANES))
    m_ref[d, hd:hd + 1, :] = b_last + mx_last


def _mlstm_scores(c, refs, p_scr, lhs_scr):
    L = CHUNK
    d, hd, reverse, (q_ref, kt_ref, v_ref, col_ref, row_ref, h_ref) = _chain_refs(c, refs)
    t_idx = lax.broadcasted_iota(jnp.int32, (L, L), 0)
    s_idx = lax.broadcasted_iota(jnp.int32, (L, L), 1)
    mask = (s_idx >= t_idx) if reverse else (s_idx <= t_idx)
    q = q_ref[:, hd * DK:(hd + 1) * DK]
    kt = kt_ref[0, hd * DK:(hd + 1) * DK, :]
    s_qk = jnp.dot(q, kt, preferred_element_type=F32)
    lhs_scr[c, :, :L] = jnp.where(mask, s_qk * p_scr[c], 0.0).astype(BF16)


def _mlstm_readout(c, refs, st_ref, lhs_scr, etb_scr, kwt_scr, wc_scr):
    L = CHUNK
    d, hd, reverse, (q_ref, kt_ref, v_ref, col_ref, row_ref, h_ref) = _chain_refs(c, refs)
    v = v_ref[:, hd * DV:(hd + 1) * DV]
    st = st_ref[d, hd]
    vext = jnp.concatenate([v, jnp.ones((L, LANES), BF16)], axis=1)
    rhs = jnp.concatenate([vext, st.astype(BF16)], axis=0)
    res = jnp.dot(lhs_scr[c], rhs, preferred_element_type=F32)
    den = jnp.maximum(jnp.abs(res[:, DV:]), etb_scr[c])
    inv = 1.0 / den
    hout = res[:, :DV] * jnp.concatenate([inv, inv], axis=1)
    h_ref[:, hd * DV:(hd + 1) * DV] = hout.astype(h_ref.dtype)
    upd = jnp.dot(kwt_scr[c], vext, preferred_element_type=F32)
    st_ref[d, hd] = wc_scr[c, 0:1, 0:1] * st + upd


def _mlstm_body(qf_ref, qb_ref, ktf_ref, ktb_ref, vf_ref, vb_ref, colf_ref, colb_ref, rowf_ref, rowb_ref,
                hf_ref, hb_ref, st_ref, m_ref, p_scr, lhs_scr, etb_scr, kwt_scr, wc_scr):
    step = pl.program_id(1)

    @pl.when(step == 0)
    def _():
        st_ref[...] = jnp.zeros(st_ref.shape, F32)
        m_ref[...] = jnp.zeros(m_ref.shape, F32)

    refs = ((qf_ref, ktf_ref, vf_ref, colf_ref, rowf_ref, hf_ref),
            (qb_ref, ktb_ref, vb_ref, colb_ref, rowb_ref, hb_ref))

    @pl.when(step >= 0)
    def _():
        for c in range(MLSTM_NCH):
            _mlstm_gates(c, refs, m_ref, p_scr, lhs_scr, etb_scr, kwt_scr, wc_scr)

    @pl.when(step >= -1)
    def _():
        for c in range(MLSTM_NCH):
            _mlstm_scores(c, refs, p_scr, lhs_scr)

    @pl.when(step >= -2)
    def _():
        for c in range(MLSTM_NCH):
            _mlstm_readout(c, refs, st_ref, lhs_scr, etb_scr, kwt_scr, wc_scr)


def _mlstm(qc, ktc, z3, gcol, grow):
    b, t, _ = qc.shape
    L = CHUNK
    ns = t // L
    vcol = ZC_V // W_V
    nch = MLSTM_NCH
    fwd3 = lambda bb, i: (bb, i, 0)
    bwd3 = lambda bb, i: (bb, ns - 1 - i, 0)
    return pl.pallas_call(
        _mlstm_body,
        grid=(b, ns),
        in_specs=[
            pl.BlockSpec((None, L, W_QK), fwd3),
            pl.BlockSpec((None, L, W_QK), bwd3),
            pl.BlockSpec((None, 1, W_QK, CHUNK), lambda bb, i: (bb, i, 0, 0)),
            pl.BlockSpec((None, 1, W_QK, CHUNK), lambda bb, i: (bb, ns - 1 - i, 0, 0)),
            pl.BlockSpec((None, L, W_V), lambda bb, i: (bb, i, vcol)),
            pl.BlockSpec((None, L, W_V), lambda bb, i: (bb, ns - 1 - i, vcol)),
            pl.BlockSpec((None, L, LANES), fwd3),
            pl.BlockSpec((None, L, LANES), bwd3),
            pl.BlockSpec((None, 2 * N_HEADS, L), lambda bb, i: (bb, 0, i)),
            pl.BlockSpec((None, 2 * N_HEADS, L), lambda bb, i: (bb, 0, ns - 1 - i)),
        ],
        out_specs=(
            pl.BlockSpec((None, L, W_V), fwd3),
            pl.BlockSpec((None, L, W_V), bwd3),
        ),
        out_shape=(
            jax.ShapeDtypeStruct((b, t, W_V), BF16),
            jax.ShapeDtypeStruct((b, t, W_V), BF16),
        ),
        scratch_shapes=[
            pltpu.VMEM((2, N_HEADS, DK, ST_W), F32),
            pltpu.VMEM((2, N_HEADS, LANES), F32),
            pltpu.VMEM((nch, L, L), F32),
            pltpu.VMEM((nch, L, 2 * L), BF16),
            pltpu.VMEM((nch, L, LANES), F32),
            pltpu.VMEM((nch, DK, L), BF16),
            pltpu.VMEM((nch, SUBLANES, LANES), F32),
        ],
        compiler_params=_cparams(("parallel", "arbitrary"), 40),
        name="mlstm",
    )(qc, qc, ktc, ktc, z3, z3, gcol, gcol, grow, grow)


FFT_N2 = 128
FFT_R = SUBLANES
FFT1_CB = W_B
FFT2_CB = 512


@functools.lru_cache(maxsize=None)
def _fft_tables(t):
    n2 = FFT_N2
    n1 = t // n2
    k1 = np.arange(n1)[:, None].astype(np.float64)
    t1 = np.arange(n1)[None, :].astype(np.float64)
    t2 = np.arange(n2)[:, None, None].astype(np.float64)
    ang = 2.0 * np.pi * (k1 * t1 / n1)[None] + 2.0 * np.pi * (k1[None] * t2 / t)
    tab1 = np.concatenate([np.cos(ang), np.sin(ang)], axis=1) / np.sqrt(n1)
    c = np.arange(GROUP_B)[:, None].astype(np.float64)
    cc = np.arange(GROUP_B)[None, :].astype(np.float64)
    angc = 2.0 * np.pi * c * cc / GROUP_B
    chan = np.concatenate([np.cos(angc), np.sin(angc)], axis=1) / np.sqrt(GROUP_B)
    k2 = np.arange(n2)[:, None].astype(np.float64)
    tt = np.arange(n2)[None, :].astype(np.float64)
    ang2 = 2.0 * np.pi * k2 * tt / n2
    tab2 = np.concatenate([np.cos(ang2), -np.sin(ang2)], axis=1) / np.sqrt(n2)
    return tuple(np.asarray(a.astype(np.float32)).astype(BF16) for a in (tab1, chan, tab2))


def _pack_complex(re, im):
    hi = lax.bitcast_convert_type(re.astype(BF16).astype(F32), jnp.uint32)
    lo = lax.bitcast_convert_type(im.astype(BF16).astype(F32), jnp.uint32)
    return hi | (lo >> 16)


def _unpack_complex(w):
    re = lax.bitcast_convert_type(w & jnp.uint32(0xFFFF0000), F32)
    im = lax.bitcast_convert_type(w << 16, F32)
    return re.astype(BF16), im.astype(BF16)


def _fft1_body(u_ref, tab_ref, e_ref, a_ref, sin_scr, uri_scr, sa_scr):
    n1 = u_ref.shape[0]
    ng = u_ref.shape[2] // LANES
    for g in range(ng):
        sin_scr[g] = u_ref[:, :, g * LANES:(g + 1) * LANES].reshape(n1 * FFT_R, LANES)
    for r in range(FFT_R):
        x = jnp.concatenate(
            [sin_scr.at[g][pl.ds(r, n1, stride=FFT_R), :] for g in range(ng)], axis=1)
        uri = jnp.dot(tab_ref[r], x.astype(BF16), preferred_element_type=F32)
        uri_scr[r] = uri.astype(BF16)
    e = e_ref[...]
    for g in range(ng):
        lhs = uri_scr[:, :, g * LANES:(g + 1) * LANES].reshape(FFT_R * 2 * n1, LANES)
        rr = jnp.dot(lhs, e, preferred_element_type=F32).reshape(FFT_R, 2 * n1, 2 * LANES)
        for r in range(FFT_R):
            ar = rr[r, :n1, :LANES] - rr[r, n1:, LANES:]
            ai = rr[r, :n1, LANES:] + rr[r, n1:, :LANES]
            sa_scr.at[g][pl.ds(r, n1, stride=FFT_R), :] = _pack_complex(ar, ai)
    for g in range(ng):
        a_ref[:, :, g * LANES:(g + 1) * LANES] = sa_scr[g].reshape(n1, FFT_R, LANES)


def _fft2_body(a_ref, tab_ref, y_ref, s_scr):
    n2 = a_ref.shape[1]
    ng = a_ref.shape[2] // LANES
    tab = tab_ref[...]
    for r in range(FFT_R):
        br, bi = _unpack_complex(a_ref[r])
        bri = jnp.concatenate([br, bi], axis=0)
        y = jnp.dot(tab, bri, preferred_element_type=F32)
        for g in range(ng):
            s_scr.at[g][pl.ds(r, n2, stride=FFT_R), :] = y[:, g * LANES:(g + 1) * LANES]
    for g in range(ng):
        y_ref[:, :, g * LANES:(g + 1) * LANES] = s_scr[g].reshape(n2, FFT_R, LANES)


def _fourier(zu3):
    b, t, c = zu3.shape
    n2 = FFT_N2
    n1 = t // n2
    tab1, chan, tab2 = _fft_tables(t)
    tab1, chan, tab2 = jnp.asarray(tab1), jnp.asarray(chan), jnp.asarray(tab2)
    u4 = zu3.reshape(b, n1, n2, c)
    cb1, cb2, r = FFT1_CB, FFT2_CB, FFT_R
    a = pl.pallas_call(
        _fft1_body,
        grid=(b, n2 // r, c // cb1),
        in_specs=[
            pl.BlockSpec((None, n1, r, cb1), lambda bb, j, k: (bb, 0, j, k)),
            pl.BlockSpec((r, 2 * n1, n1), lambda bb, j, k: (j, 0, 0)),
            pl.BlockSpec((GROUP_B, 2 * GROUP_B), lambda bb, j, k: (0, 0)),
        ],
        out_specs=pl.BlockSpec((None, n1, r, cb1), lambda bb, j, k: (bb, 0, j, k)),
        out_shape=jax.ShapeDtypeStruct((b, n1, n2, c), jnp.uint32),
        scratch_shapes=[
            pltpu.VMEM((cb1 // LANES, n1 * r, LANES), F32),
            pltpu.VMEM((r, 2 * n1, cb1), BF16),
            pltpu.VMEM((cb1 // LANES, n1 * r, LANES), jnp.uint32),
        ],
        compiler_params=_cparams(("parallel", "parallel", "parallel"), 48),
        name="fft_stage1",
    )(u4, tab1, chan)
    y = pl.pallas_call(
        _fft2_body,
        grid=(b, n1 // r, c // cb2),
        in_specs=[
            pl.BlockSpec((None, r, n2, cb2), lambda bb, j, k: (bb, j, 0, k)),
            pl.BlockSpec((n2, 2 * n2), lambda bb, j, k: (0, 0)),
        ],
        out_specs=pl.BlockSpec((None, n2, r, cb2), lambda bb, j, k: (bb, 0, j, k)),
        out_shape=jax.ShapeDtypeStruct((b, n2, n1, c), F32),
        scratch_shapes=[pltpu.VMEM((cb2 // LANES, n2 * r, LANES), F32)],
        compiler_params=_cparams(("parallel", "parallel", "parallel"), 40),
        name="fft_stage2",
    )(a, tab2)
    return y.reshape(b, t, c)


OUT_TM = 256


def _rowsum_lanes(x, ones_bf16):
    return jnp.dot(x.astype(BF16), ones_bf16, preferred_element_type=F32)


def _outproj_body(hf_ref, hb_ref, o_ref, gh_ref, fb_ref, ma_ref, mb_ref, bma_ref, bmb_ref, x_ref,
                  wa_ref, wb_ref, wo_ref, gpost_ref, out_ref):
    ones = jnp.ones((DV, LANES), BF16)
    ha = hf_ref[...].astype(F32) + hb_ref[...].astype(F32)
    parts = []
    for hd in range(N_HEADS):
        hh = ha[:, hd * DV:(hd + 1) * DV]
        ssum = _rowsum_lanes(hh * hh, ones)
        inv = lax.rsqrt(ssum * (1.0 / DV) + EPS)
        parts.append(hh * jnp.concatenate([inv, inv], axis=1))
    han = jnp.concatenate(parts, axis=1) * gh_ref[...]
    han = han * jax.nn.sigmoid(o_ref[...].astype(F32))
    ya = jnp.dot(han.astype(BF16), wa_ref[...], preferred_element_type=F32)
    yb = jnp.dot(fb_ref[...].astype(BF16), wb_ref[...], preferred_element_type=F32)
    ga = jax.nn.sigmoid(ma_ref[...].astype(F32) + bma_ref[...])
    gb = jax.nn.sigmoid(mb_ref[...].astype(F32) + bmb_ref[...])
    mixin = (ga * ya + gb * yb).astype(BF16)
    mix = jnp.dot(mixin, wo_ref[...], preferred_element_type=F32)
    ms = jnp.mean(mix * mix, axis=-1, keepdims=True)
    out_ref[...] = x_ref[...] + mix * lax.rsqrt(ms + EPS) * gpost_ref[...]


def _outproj(hf, hb, z, g_head, fb, b_merge, x2d, wa, wb, wo, g_post):
    m = x2d.shape[0]
    tm = OUT_TM
    row = lambda i: (i, 0)
    const = lambda i: (0, 0)
    resident = functools.partial(pl.BlockSpec, index_map=const, pipeline_mode=pl.Buffered(1))
    return pl.pallas_call(
        _outproj_body,
        grid=(m // tm,),
        in_specs=[
            pl.BlockSpec((tm, W_V), row),
            pl.BlockSpec((tm, W_V), row),
            pl.BlockSpec((tm, W_V), lambda i: (i, ZC_O // W_V)),
            pl.BlockSpec((1, W_V), const),
            pl.BlockSpec((tm, W_B), row),
            pl.BlockSpec((tm, D_MODEL), lambda i: (i, ZC_MA // D_MODEL)),
            pl.BlockSpec((tm, D_MODEL), lambda i: (i, ZC_MB // D_MODEL)),
            pl.BlockSpec((1, D_MODEL), lambda i: (0, 0)),
            pl.BlockSpec((1, D_MODEL), lambda i: (0, 1)),
            pl.BlockSpec((tm, D_MODEL), row),
            resident((W_V, D_MODEL)),
            resident((W_B, D_MODEL)),
            resident((D_MODEL, D_MODEL)),
            pl.BlockSpec((1, D_MODEL), const),
        ],
        out_specs=pl.BlockSpec((tm, D_MODEL), row),
        out_shape=jax.ShapeDtypeStruct((m, D_MODEL), F32),
        compiler_params=_cparams(("parallel",), 56),
        name="outproj",
    )(hf, hb, z, g_head, fb, z, z, b_merge, b_merge, x2d, wa, wb, wo, g_post)


FFN_TM = 1024
FFN_TF = 256


def _ffn_body(x_ref, gpre_ref, wg_ref, wu_ref, wo_ref, gpost_ref, out_ref, h_scr):
    j = pl.program_id(1)

    @pl.when(j == 0)
    def _():
        x = x_ref[...]
        ms = jnp.mean(x * x, axis=-1, keepdims=True)
        h_scr[...] = (x * lax.rsqrt(ms + EPS) * gpre_ref[...]).astype(BF16)
        out_ref[...] = jnp.zeros(out_ref.shape, F32)

    h = h_scr[...]
    g = jnp.dot(h, wg_ref[...], preferred_element_type=F32)
    u = jnp.dot(h, wu_ref[...], preferred_element_type=F32)
    act = (g * jax.nn.sigmoid(g) * u).astype(BF16)
    out_ref[...] += jnp.dot(act, wo_ref[...], preferred_element_type=F32)

    @pl.when(j == pl.num_programs(1) - 1)
    def _():
        ff = out_ref[...]
        ms = jnp.mean(ff * ff, axis=-1, keepdims=True)
        out_ref[...] = x_ref[...] + ff * lax.rsqrt(ms + EPS) * gpost_ref[...]


def _ffn(x2d, g_pre, w_in, w_out, g_post):
    m = x2d.shape[0]
    tm, tf = FFN_TM, FFN_TF
    nf = D_FF // tf
    return pl.pallas_call(
        _ffn_body,
        grid=(m // tm, nf),
        in_specs=[
            pl.BlockSpec((tm, D_MODEL), lambda i, j: (i, 0)),
            pl.BlockSpec((1, D_MODEL), lambda i, j: (0, 0)),
            pl.BlockSpec((D_MODEL, tf), lambda i, j: (0, j)),
            pl.BlockSpec((D_MODEL, tf), lambda i, j: (0, j + nf)),
            pl.BlockSpec((tf, D_MODEL), lambda i, j: (j, 0)),
            pl.BlockSpec((1, D_MODEL), lambda i, j: (0, 0)),
        ],
        out_specs=pl.BlockSpec((tm, D_MODEL), lambda i, j: (i, 0)),
        out_shape=jax.ShapeDtypeStruct((m, D_MODEL), F32),
        scratch_shapes=[pltpu.VMEM((tm, D_MODEL), BF16)],
        compiler_params=_cparams(("parallel", "arbitrary"), 60),
        name="ffn",
    )(x2d, g_pre, w_in, w_in, w_out, g_post)


def _prep_layer(w_in, conv_w, b_gates, g_head, w_a_out, w_b_out, b_merge, w_out,
                w_ffn_in, w_ffn_out):
    w_main = jnp.concatenate([w_in[:, :OFF_G], w_in[:, OFF_M:], w_in[:, OFF_B:OFF_M]], axis=1).astype(BF16)
    w_gate = jnp.pad(w_in[:, OFF_G:OFF_B], ((0, 0), (0, LANES - N_GATES))).astype(BF16)
    conv_w8 = jnp.pad(conv_w, ((0, SUBLANES - CONV_W), (0, 0)))
    bias_row = jnp.pad(b_gates[None, :], ((0, 0), (0, LANES - N_GATES)))
    return dict(
        w_main=w_main, w_gate=w_gate, conv_w8=conv_w8, bias_row=bias_row,
        g_head=g_head[None, :], wa=w_a_out.astype(BF16), wb=w_b_out.astype(BF16),
        b_merge=b_merge[None, :], wo=w_out.astype(BF16),
        w_ffn_in=w_ffn_in.astype(BF16), w_ffn_out=w_ffn_out.astype(BF16),
    )


def _layer(x, p, g_pre_mix, g_post_mix, g_pre_ffn, g_post_ffn):
    b, t, d = x.shape
    m = b * t
    x2d = x.reshape(m, d)
    z, zu, zg = _inproj(x2d, g_pre_mix[None, :], p["w_main"], p["w_gate"])
    z3 = z.reshape(b, t, Z_W)
    qc = _conv_silu(z3, p["conv_w8"], keys=False)
    ktc = _conv_silu(z3, p["conv_w8"], keys=True)
    gcol, grow = _gate_scans(zg.reshape(b, t, LANES), p["bias_row"])
    hf, hb = _mlstm(qc, ktc, z3, gcol, grow)
    fb = _fourier(zu.reshape(b, t, W_B))
    x1 = _outproj(hf.reshape(m, W_V), hb.reshape(m, W_V), z, p["g_head"], fb.reshape(m, W_B),
                  p["b_merge"], x2d, p["wa"], p["wb"], p["wo"], g_post_mix[None, :])
    y = _ffn(x1, g_pre_ffn[None, :], p["w_ffn_in"], p["w_ffn_out"], g_post_ffn[None, :])
    return y.reshape(b, t, d)


def kernel(x_prompt, x_sample, g_pre_mix, w_in, conv_w, b_gates, g_head, w_a_out, w_b_out, b_merge,
           w_out, g_post_mix, g_pre_ffn, w_ffn_in, w_ffn_out, g_post_ffn):
    depth = w_in.shape[0]
    layers = [
        _prep_layer(w_in[l], conv_w[l], b_gates[l], g_head[l], w_a_out[l], w_b_out[l], b_merge[l],
                    w_out[l], w_ffn_in[l], w_ffn_out[l])
        for l in range(depth)
    ]

    def trunk(x):
        for l in range(depth):
            x = _layer(x, layers[l], g_pre_mix[l], g_post_mix[l], g_pre_ffn[l], g_post_ffn[l])
        return x

    return (trunk(x_prompt), trunk(x_sample))
```

```python
import functools

import numpy as np
import jax
import jax.numpy as jnp
from jax import lax
from jax.experimental import pallas as pl
from jax.experimental.pallas import tpu as pltpu

F32 = jnp.float32
BF16 = jnp.bfloat16

D_MODEL = 2048
N_HEADS = 8
DK = 128
DV = 256
W_QK = N_HEADS * DK
W_V = N_HEADS * DV
N_GATES = 4 * N_HEADS
CONV_W = 5
CHUNK = 128
N_GROUPS_B = 8
GROUP_B = 128
W_B = N_GROUPS_B * GROUP_B
D_FF = 5632
OFF_V = 2 * W_QK
OFF_O = OFF_V + W_V
OFF_G = OFF_O + W_V
OFF_B = OFF_G + N_GATES
OFF_M = OFF_B + W_B
EPS = 1e-6

LANES = 128
SUBLANES = 8
MIB = 1024 * 1024


def _cparams(sem, vmem_mib):
    return pltpu.CompilerParams(dimension_semantics=sem, vmem_limit_bytes=vmem_mib * MIB)


IN_TM = 1024
IN_TN = 1024
ZC_QK = 0
ZC_V = ZC_QK + 2 * W_QK
ZC_O = ZC_V + W_V
ZC_MA = ZC_O + W_V
ZC_MB = ZC_MA + D_MODEL
ZC_U = ZC_MB + D_MODEL
Z_W = ZC_U + W_B
IN_NJ = Z_W // IN_TN
assert IN_TN == W_B and ZC_U == (IN_NJ - 1) * IN_TN


def _inproj_body(x_ref, g_ref, w_ref, wg_ref, z_ref, zu_ref, zg_ref, h_scr):
    j = pl.program_id(1)

    @pl.when(j == 0)
    def _():
        x = x_ref[...]
        ms = jnp.mean(x * x, axis=-1, keepdims=True)
        h = (x * lax.rsqrt(ms + EPS) * g_ref[...]).astype(BF16)
        h_scr[...] = h
        zg_ref[...] = jnp.dot(h, wg_ref[...], preferred_element_type=F32)

    acc = jnp.dot(h_scr[...], w_ref[...], preferred_element_type=F32)
    z_ref[...] = acc.astype(BF16)
    zu_ref[...] = acc


def _inproj(x2d, g, w_main, w_gate):
    m = x2d.shape[0]
    tm, tn = IN_TM, IN_TN
    out_shapes = (
        jax.ShapeDtypeStruct((m, Z_W), BF16),
        jax.ShapeDtypeStruct((m, W_B), F32),
        jax.ShapeDtypeStruct((m, LANES), F32),
    )
    return pl.pallas_call(
        _inproj_body,
        grid=(m // tm, IN_NJ),
        in_specs=[
            pl.BlockSpec((tm, D_MODEL), lambda i, j: (i, 0)),
            pl.BlockSpec((1, D_MODEL), lambda i, j: (0, 0)),
            pl.BlockSpec((D_MODEL, tn), lambda i, j: (0, j)),
            pl.BlockSpec((D_MODEL, LANES), lambda i, j: (0, 0)),
        ],
        out_specs=(
            pl.BlockSpec((tm, tn), lambda i, j: (i, j)),
            pl.BlockSpec((tm, W_B), lambda i, j: (i, 0)),
            pl.BlockSpec((tm, LANES), lambda i, j: (i, 0)),
        ),
        out_shape=out_shapes,
        scratch_shapes=[pltpu.VMEM((tm, D_MODEL), BF16)],
        compiler_params=_cparams(("parallel", "arbitrary"), 56),
        name="inproj",
    )(x2d, g, w_main, w_gate)


CONV_TQ = 512
CONV_CW = 512
CONV_HALO = 64
CONV_NQ = W_QK // CONV_CW
CONV_WIN = 2 * CHUNK
CONV_TAPS = tuple(d for d in range(-(CONV_W // 2), CONV_W // 2 + 1) if d != 0)


@functools.lru_cache(maxsize=None)
def _shift_matrix():
    m = np.zeros((len(CONV_TAPS) * CHUNK, CONV_WIN), np.float32)
    for k, d in enumerate(CONV_TAPS):
        m[k * CHUNK + np.arange(CHUNK), CONV_HALO + np.arange(CHUNK) + d] = 1.0
    return m.astype(BF16)


def _conv_body(prev_ref, cur_ref, next_ref, w_ref, s_ref, out_ref, *, keys):
    i = pl.program_id(1)
    n = pl.num_programs(1)
    tq = cur_ref.shape[0]
    hl = CONV_HALO
    pv = jnp.where(i > 0, prev_ref[...], jnp.zeros_like(prev_ref[...]))
    nx = jnp.where(i < n - 1, next_ref[...], jnp.zeros_like(next_ref[...]))
    ext = jnp.concatenate([pv, cur_ref[...], nx], axis=0)
    shift = s_ref[...]
    parts = []
    for r in range(tq // CHUNK):
        win = ext[r * CHUNK:r * CHUNK + CONV_WIN]
        sh = jnp.dot(shift, win, preferred_element_type=F32)
        acc = win[hl:hl + CHUNK].astype(F32) * w_ref[CONV_W // 2:CONV_W // 2 + 1, :]
        for k, d in enumerate(CONV_TAPS):
            j = d + CONV_W // 2
            acc = acc + sh[k * CHUNK:(k + 1) * CHUNK] * w_ref[j:j + 1, :]
        parts.append(acc * jax.nn.sigmoid(acc))
    if keys:
        for c, y in enumerate(parts):
            out_ref[c] = (y * (DK ** -0.5)).T.astype(BF16)
    else:
        for c, y in enumerate(parts):
            out_ref[c * CHUNK:(c + 1) * CHUNK, :] = y.astype(BF16)


def _conv_silu(z3, conv_w8, keys):
    b, t, _ = z3.shape
    tq, cw, hl = CONV_TQ, CONV_CW, CONV_HALO
    r = tq // hl
    nh = t // hl
    c0 = CONV_NQ if keys else 0
    if keys:
        out_spec = pl.BlockSpec((None, tq // CHUNK, cw, CHUNK), lambda bb, i, k: (bb, i, k, 0))
        out_shape = jax.ShapeDtypeStruct((b, t // CHUNK, W_QK, CHUNK), BF16)
    else:
        out_spec = pl.BlockSpec((None, tq, cw), lambda bb, i, k: (bb, i, k))
        out_shape = jax.ShapeDtypeStruct((b, t, W_QK), BF16)
    shift = jnp.asarray(_shift_matrix())
    return pl.pallas_call(
        functools.partial(_conv_body, keys=keys),
        grid=(b, t // tq, CONV_NQ),
        in_specs=[
            pl.BlockSpec((None, hl, cw), lambda bb, i, k: (bb, jnp.maximum(i * r - 1, 0), c0 + k)),
            pl.BlockSpec((None, tq, cw), lambda bb, i, k: (bb, i, c0 + k)),
            pl.BlockSpec((None, hl, cw), lambda bb, i, k: (bb, jnp.minimum((i + 1) * r, nh - 1), c0 + k)),
            pl.BlockSpec((SUBLANES, cw), lambda bb, i, k: (0, c0 + k)),
            pl.BlockSpec(shift.shape, lambda bb, i, k: (0, 0)),
        ],
        out_specs=out_spec,
        out_shape=out_shape,
        compiler_params=_cparams(("parallel", "parallel", "parallel"), 40),
        name="conv_silu_k" if keys else "conv_silu_q",
    )(z3, z3, z3, conv_w8, shift)


GATE_TG = 1024


def _log_sigmoid(x):
    return jnp.minimum(x, 0.0) - jnp.log1p(jnp.exp(-jnp.abs(x)))


def _gates_body(zg_ref, bias_ref, col_ref, row_ref):
    g = zg_ref[...] + bias_ref[...]
    gt = g.T
    h = N_HEADS
    li_f, lf_f = gt[0:h], _log_sigmoid(gt[h:2 * h])
    li_b, lf_b = gt[2 * h:3 * h], _log_sigmoid(gt[3 * h:4 * h])
    tg = gt.shape[1]
    pos = lax.broadcasted_iota(jnp.int32, (h, tg), 1) & (CHUNK - 1)

    def scan(x, op, fill, reverse):
        s = 1
        while s < CHUNK:
            if reverse:
                sh = pltpu.roll(x, tg - s, 1)
                ok = pos < CHUNK - s
            else:
                sh = pltpu.roll(x, s, 1)
                ok = pos >= s
            x = op(x, jnp.where(ok, sh, fill))
            s *= 2
        return x

    b_f = scan(lf_f, jnp.add, 0.0, False)
    b_b = scan(lf_b, jnp.add, 0.0, True)
    w_f = li_f - b_f
    w_b = li_b - b_b
    a_f = scan(w_f, jnp.maximum, -jnp.inf, False)
    a_b = scan(w_b, jnp.maximum, -jnp.inf, True)
    row_ref[...] = jnp.concatenate([w_f, w_b], axis=0)
    pad = jnp.zeros((LANES - 4 * h, tg), F32)
    col_ref[...] = jnp.concatenate([b_f, a_f, b_b, a_b, pad], axis=0).T


def _gate_scans(zg3, bias_row):
    b, t, _ = zg3.shape
    tg = GATE_TG
    return pl.pallas_call(
        _gates_body,
        grid=(b, t // tg),
        in_specs=[
            pl.BlockSpec((None, tg, LANES), lambda bb, i: (bb, i, 0)),
            pl.BlockSpec((1, LANES), lambda bb, i: (0, 0)),
        ],
        out_specs=(
            pl.BlockSpec((None, tg, LANES), lambda bb, i: (bb, i, 0)),
            pl.BlockSpec((None, 2 * N_HEADS, tg), lambda bb, i: (bb, 0, i)),
        ),
        out_shape=(
            jax.ShapeDtypeStruct((b, t, LANES), F32),
            jax.ShapeDtypeStruct((b, 2 * N_HEADS, t), F32),
        ),
        compiler_params=_cparams(("parallel", "parallel"), 32),
        name="gate_scans",
    )(zg3, bias_row)


ST_W = DV + LANES
MLSTM_NCH = 2 * N_HEADS


def _chain_refs(c, refs):
    d, hd = divmod(c, N_HEADS)
    return d, hd, d == 1, refs[d]


def _mlstm_gates(c, refs, m_ref, p_scr, lhs_scr, etb_scr, kwt_scr, wc_scr):
    L = CHUNK
    d, hd, reverse, (q_ref, kt_ref, v_ref, col_ref, row_ref, h_ref) = _chain_refs(c, refs)
    last = 0 if reverse else L - 1
    cb = 2 * N_HEADS * d + hd
    colblk = col_ref[...]
    lane_b = jnp.full((L, LANES), cb, jnp.int32)
    b_b = jnp.take_along_axis(colblk, lane_b, axis=1)
    a_b = jnp.take_along_axis(colblk, lane_b + N_HEADS, axis=1)
    wrow = row_ref[N_HEADS * d + hd:N_HEADS * d + hd + 1, :]
    m11 = m_ref[d, hd:hd + 1, 0:1]
    mxb = jnp.maximum(a_b, m11)
    mx_last = mxb[last:last + 1, :]
    b_last = b_b[last:last + 1, :]
    p_scr[c] = jnp.exp(wrow - mxb)
    q = q_ref[:, hd * DK:(hd + 1) * DK]
    lhs_scr[c, :, L:] = (q.astype(F32) * jnp.exp(m11 - mxb)).astype(BF16)
    etb_scr[c] = jnp.exp(-(b_b + mxb))
    kt = kt_ref[0, hd * DK:(hd + 1) * DK, :]
    kwt_scr[c] = (kt.astype(F32) * jnp.exp(wrow - mx_last)).astype(BF16)
    wc_scr[c] = jnp.broadcast_to(jnp.exp(m11 - mx_last), (SUBLANES, LANES))
    m_ref[d, hd:hd + 1, :] = b_last + mx_last


def _mlstm_scores(c, refs, p_scr, lhs_scr):
    L = CHUNK
    d, hd, reverse, (q_ref, kt_ref, v_ref, col_ref, row_ref, h_ref) = _chain_refs(c, refs)
    t_idx = lax.broadcasted_iota(jnp.int32, (L, L), 0)
    s_idx = lax.broadcasted_iota(jnp.int32, (L, L), 1)
    mask = (s_idx >= t_idx) if reverse else (s_idx <= t_idx)
    q = q_ref[:, hd * DK:(hd + 1) * DK]
    kt = kt_ref[0, hd * DK:(hd + 1) * DK, :]
    s_qk = jnp.dot(q, kt, preferred_element_type=F32)
    lhs_scr[c, :, :L] = jnp.where(mask, s_qk * p_scr[c], 0.0).astype(BF16)


def _mlstm_readout(c, refs, st_ref, lhs_scr, etb_scr, kwt_scr, wc_scr):
    L = CHUNK
    d, hd, reverse, (q_ref, kt_ref, v_ref, col_ref, row_ref, h_ref) = _chain_refs(c, refs)
    v = v_ref[:, hd * DV:(hd + 1) * DV]
    st = st_ref[d, hd]
    vext = jnp.concatenate([v, jnp.ones((L, LANES), BF16)], axis=1)
    rhs = jnp.concatenate([vext, st.astype(BF16)], axis=0)
    res = jnp.dot(lhs_scr[c], rhs, preferred_element_type=F32)
    den = jnp.maximum(jnp.abs(res[:, DV:]), etb_scr[c])
    inv = 1.0 / den
    hout = res[:, :DV] * jnp.concatenate([inv, inv], axis=1)
    h_ref[:, hd * DV:(hd + 1) * DV] = hout.astype(h_ref.dtype)
    upd = jnp.dot(kwt_scr[c], vext, preferred_element_type=F32)
    st_ref[d, hd] = wc_scr[c, 0:1, 0:1] * st + upd


def _mlstm_body(qf_ref, qb_ref, ktf_ref, ktb_ref, vf_ref, vb_ref, colf_ref, colb_ref, rowf_ref, rowb_ref,
                hf_ref, hb_ref, st_ref, m_ref, p_scr, lhs_scr, etb_scr, kwt_scr, wc_scr):
    step = pl.program_id(1)

    @pl.when(step == 0)
    def _():
        st_ref[...] = jnp.zeros(st_ref.shape, F32)
        m_ref[...] = jnp.zeros(m_ref.shape, F32)

    refs = ((qf_ref, ktf_ref, vf_ref, colf_ref, rowf_ref, hf_ref),
            (qb_ref, ktb_ref, vb_ref, colb_ref, rowb_ref, hb_ref))

    @pl.when(step >= 0)
    def _():
        for c in range(MLSTM_NCH):
            _mlstm_gates(c, refs, m_ref, p_scr, lhs_scr, etb_scr, kwt_scr, wc_scr)

    @pl.when(step >= -1)
    def _():
        for c in range(MLSTM_NCH):
            _mlstm_scores(c, refs, p_scr, lhs_scr)

    @pl.when(step >= -2)
    def _():
        for c in range(MLSTM_NCH):
            _mlstm_readout(c, refs, st_ref, lhs_scr, etb_scr, kwt_scr, wc_scr)


def _mlstm(qc, ktc, z3, gcol, grow):
    b, t, _ = qc.shape
    L = CHUNK
    ns = t // L
    vcol = ZC_V // W_V
    nch = MLSTM_NCH
    fwd3 = lambda bb, i: (bb, i, 0)
    bwd3 = lambda bb, i: (bb, ns - 1 - i, 0)
    return pl.pallas_call(
        _mlstm_body,
        grid=(b, ns),
        in_specs=[
            pl.BlockSpec((None, L, W_QK), fwd3),
            pl.BlockSpec((None, L, W_QK), bwd3),
            pl.BlockSpec((None, 1, W_QK, CHUNK), lambda bb, i: (bb, i, 0, 0)),
            pl.BlockSpec((None, 1, W_QK, CHUNK), lambda bb, i: (bb, ns - 1 - i, 0, 0)),
            pl.BlockSpec((None, L, W_V), lambda bb, i: (bb, i, vcol)),
            pl.BlockSpec((None, L, W_V), lambda bb, i: (bb, ns - 1 - i, vcol)),
            pl.BlockSpec((None, L, LANES), fwd3),
            pl.BlockSpec((None, L, LANES), bwd3),
            pl.BlockSpec((None, 2 * N_HEADS, L), lambda bb, i: (bb, 0, i)),
            pl.BlockSpec((None, 2 * N_HEADS, L), lambda bb, i: (bb, 0, ns - 1 - i)),
        ],
        out_specs=(
            pl.BlockSpec((None, L, W_V), fwd3),
            pl.BlockSpec((None, L, W_V), bwd3),
        ),
        out_shape=(
            jax.ShapeDtypeStruct((b, t, W_V), BF16),
            jax.ShapeDtypeStruct((b, t, W_V), BF16),
        ),
        scratch_shapes=[
            pltpu.VMEM((2, N_HEADS, DK, ST_W), F32),
            pltpu.VMEM((2, N_HEADS, LANES), F32),
            pltpu.VMEM((nch, L, L), F32),
            pltpu.VMEM((nch, L, 2 * L), BF16),
            pltpu.VMEM((nch, L, LANES), F32),
            pltpu.VMEM((nch, DK, L), BF16),
            pltpu.VMEM((nch, SUBLANES, LANES), F32),
        ],
        compiler_params=_cparams(("parallel", "arbitrary"), 40),
        name="mlstm",
    )(qc, qc, ktc, ktc, z3, z3, gcol, gcol, grow, grow)


FFT_N2 = 128
FFT_R = SUBLANES
FFT1_CB = W_B
FFT2_CB = 512


@functools.lru_cache(maxsize=None)
def _fft_tables(t):
    n2 = FFT_N2
    n1 = t // n2
    k1 = np.arange(n1)[:, None].astype(np.float64)
    t1 = np.arange(n1)[None, :].astype(np.float64)
    t2 = np.arange(n2)[:, None, None].astype(np.float64)
    ang = 2.0 * np.pi * (k1 * t1 / n1)[None] + 2.0 * np.pi * (k1[None] * t2 / t)
    tab1 = np.concatenate([np.cos(ang), np.sin(ang)], axis=1) / np.sqrt(n1)
    c = np.arange(GROUP_B)[:, None].astype(np.float64)
    cc = np.arange(GROUP_B)[None, :].astype(np.float64)
    angc = 2.0 * np.pi * c * cc / GROUP_B
    cg, sg = np.cos(angc) / np.sqrt(GROUP_B), np.sin(angc) / np.sqrt(GROUP_B)
    chan = np.block([[cg, sg], [-sg, cg]])
    k2 = np.arange(n2)[:, None].astype(np.float64)
    tt = np.arange(n2)[None, :].astype(np.float64)
    ang2 = 2.0 * np.pi * k2 * tt / n2
    tab2 = np.concatenate([np.cos(ang2), -np.sin(ang2)], axis=1) / np.sqrt(n2)
    return tuple(np.asarray(a.astype(np.float32)).astype(BF16) for a in (tab1, chan, tab2))


def _pack_complex(re, im):
    hi = lax.bitcast_convert_type(re.astype(BF16).astype(F32), jnp.uint32)
    lo = lax.bitcast_convert_type(im.astype(BF16).astype(F32), jnp.uint32)
    return hi | (lo >> 16)


def _unpack_complex(w):
    re = lax.bitcast_convert_type(w & jnp.uint32(0xFFFF0000), F32)
    im = lax.bitcast_convert_type(w << 16, F32)
    return re.astype(BF16), im.astype(BF16)


def _fft1_body(u_ref, tab_ref, e_ref, a_ref, sin_scr, uri_scr, sa_scr):
    n1 = u_ref.shape[0]
    ng = u_ref.shape[2] // LANES
    for g in range(ng):
        sin_scr[g] = u_ref[:, :, g * LANES:(g + 1) * LANES].reshape(n1 * FFT_R, LANES)
    for r in range(FFT_R):
        x = jnp.concatenate(
            [sin_scr.at[g][pl.ds(r, n1, stride=FFT_R), :] for g in range(ng)], axis=1)
        uri = jnp.dot(tab_ref[r], x.astype(BF16), preferred_element_type=F32)
        uri_scr[r] = uri.astype(BF16)
    e = e_ref[...]
    for g in range(ng):
        lanes = slice(g * LANES, (g + 1) * LANES)
        lhs = jnp.concatenate([uri_scr[:, :n1, lanes], uri_scr[:, n1:, lanes]], axis=-1)
        rr = jnp.dot(lhs.reshape(FFT_R * n1, 2 * LANES), e, preferred_element_type=F32)
        rr = rr.reshape(FFT_R, n1, 2 * LANES)
        for r in range(FFT_R):
            sa_scr.at[g][pl.ds(r, n1, stride=FFT_R), :] = _pack_complex(rr[r, :, :LANES], rr[r, :, LANES:])
    for g in range(ng):
        a_ref[:, :, g * LANES:(g + 1) * LANES] = sa_scr[g].reshape(n1, FFT_R, LANES)


def _fft2_body(a_ref, tab_ref, y_ref, s_scr):
    n2 = a_ref.shape[1]
    ng = a_ref.shape[2] // LANES
    tab = tab_ref[...]
    for r in range(FFT_R):
        br, bi = _unpack_complex(a_ref[r])
        bri = jnp.concatenate([br, bi], axis=0)
        y = jnp.dot(tab, bri, preferred_element_type=F32)
        for g in range(ng):
            s_scr.at[g][pl.ds(r, n2, stride=FFT_R), :] = y[:, g * LANES:(g + 1) * LANES]
    for g in range(ng):
        y_ref[:, :, g * LANES:(g + 1) * LANES] = s_scr[g].reshape(n2, FFT_R, LANES)


def _fourier(zu3):
    b, t, c = zu3.shape
    n2 = FFT_N2
    n1 = t // n2
    tab1, chan, tab2 = _fft_tables(t)
    tab1, chan, tab2 = jnp.asarray(tab1), jnp.asarray(chan), jnp.asarray(tab2)
    u4 = zu3.reshape(b, n1, n2, c)
    cb1, cb2, r = FFT1_CB, FFT2_CB, FFT_R
    a = pl.pallas_call(
        _fft1_body,
        grid=(b, n2 // r, c // cb1),
        in_specs=[
            pl.BlockSpec((None, n1, r, cb1), lambda bb, j, k: (bb, 0, j, k)),
            pl.BlockSpec((r, 2 * n1, n1), lambda bb, j, k: (j, 0, 0)),
            pl.BlockSpec((2 * GROUP_B, 2 * GROUP_B), lambda bb, j, k: (0, 0)),
        ],
        out_specs=pl.BlockSpec((None, n1, r, cb1), lambda bb, j, k: (bb, 0, j, k)),
        out_shape=jax.ShapeDtypeStruct((b, n1, n2, c), jnp.uint32),
        scratch_shapes=[
            pltpu.VMEM((cb1 // LANES, n1 * r, LANES), F32),
            pltpu.VMEM((r, 2 * n1, cb1), BF16),
            pltpu.VMEM((cb1 // LANES, n1 * r, LANES), jnp.uint32),
        ],
        compiler_params=_cparams(("parallel", "parallel", "parallel"), 48),
        name="fft_stage1",
    )(u4, tab1, chan)
    y = pl.pallas_call(
        _fft2_body,
        grid=(b, n1 // r, c // cb2),
        in_specs=[
            pl.BlockSpec((None, r, n2, cb2), lambda bb, j, k: (bb, j, 0, k)),
            pl.BlockSpec((n2, 2 * n2), lambda bb, j, k: (0, 0)),
        ],
        out_specs=pl.BlockSpec((None, n2, r, cb2), lambda bb, j, k: (bb, 0, j, k)),
        out_shape=jax.ShapeDtypeStruct((b, n2, n1, c), F32),
        scratch_shapes=[pltpu.VMEM((cb2 // LANES, n2 * r, LANES), F32)],
        compiler_params=_cparams(("parallel", "parallel", "parallel"), 40),
        name="fft_stage2",
    )(a, tab2)
    return y.reshape(b, t, c)


OUT_TM = 256


def _rowsum_lanes(x, ones_bf16):
    return jnp.dot(x.astype(BF16), ones_bf16, preferred_element_type=F32)


def _outproj_body(hf_ref, hb_ref, o_ref, gh_ref, fb_ref, ma_ref, mb_ref, bma_ref, bmb_ref, x_ref,
                  wa_ref, wb_ref, wo_ref, gpost_ref, out_ref):
    ones = jnp.ones((DV, LANES), BF16)
    ha = hf_ref[...].astype(F32) + hb_ref[...].astype(F32)
    parts = []
    for hd in range(N_HEADS):
        hh = ha[:, hd * DV:(hd + 1) * DV]
        ssum = _rowsum_lanes(hh * hh, ones)
        inv = lax.rsqrt(ssum * (1.0 / DV) + EPS)
        parts.append(hh * jnp.concatenate([inv, inv], axis=1))
    han = jnp.concatenate(parts, axis=1) * gh_ref[...]
    han = han * jax.nn.sigmoid(o_ref[...].astype(F32))
    ya = jnp.dot(han.astype(BF16), wa_ref[...], preferred_element_type=F32)
    yb = jnp.dot(fb_ref[...].astype(BF16), wb_ref[...], preferred_element_type=F32)
    ga = jax.nn.sigmoid(ma_ref[...].astype(F32) + bma_ref[...])
    gb = jax.nn.sigmoid(mb_ref[...].astype(F32) + bmb_ref[...])
    mixin = (ga * ya + gb * yb).astype(BF16)
    mix = jnp.dot(mixin, wo_ref[...], preferred_element_type=F32)
    ms = jnp.mean(mix * mix, axis=-1, keepdims=True)
    out_ref[...] = x_ref[...] + mix * lax.rsqrt(ms + EPS) * gpost_ref[...]


def _outproj(hf, hb, z, g_head, fb, b_merge, x2d, wa, wb, wo, g_post):
    m = x2d.shape[0]
    tm = OUT_TM
    row = lambda i: (i, 0)
    const = lambda i: (0, 0)
    resident = functools.partial(pl.BlockSpec, index_map=const, pipeline_mode=pl.Buffered(1))
    return pl.pallas_call(
        _outproj_body,
        grid=(m // tm,),
        in_specs=[
            pl.BlockSpec((tm, W_V), row),
            pl.BlockSpec((tm, W_V), row),
            pl.BlockSpec((tm, W_V), lambda i: (i, ZC_O // W_V)),
            pl.BlockSpec((1, W_V), const),
            pl.BlockSpec((tm, W_B), row),
            pl.BlockSpec((tm, D_MODEL), lambda i: (i, ZC_MA // D_MODEL)),
            pl.BlockSpec((tm, D_MODEL), lambda i: (i, ZC_MB // D_MODEL)),
            pl.BlockSpec((1, D_MODEL), lambda i: (0, 0)),
            pl.BlockSpec((1, D_MODEL), lambda i: (0, 1)),
            pl.BlockSpec((tm, D_MODEL), row),
            resident((W_V, D_MODEL)),
            resident((W_B, D_MODEL)),
            resident((D_MODEL, D_MODEL)),
            pl.BlockSpec((1, D_MODEL), const),
        ],
        out_specs=pl.BlockSpec((tm, D_MODEL), row),
        out_shape=jax.ShapeDtypeStruct((m, D_MODEL), F32),
        compiler_params=_cparams(("parallel",), 56),
        name="outproj",
    )(hf, hb, z, g_head, fb, z, z, b_merge, b_merge, x2d, wa, wb, wo, g_post)


FFN_TM = 1024
FFN_TF = 256


def _ffn_body(x_ref, gpre_ref, wg_ref, wu_ref, wo_ref, gpost_ref, out_ref, h_scr):
    j = pl.program_id(1)

    @pl.when(j == 0)
    def _():
        x = x_ref[...]
        ms = jnp.mean(x * x, axis=-1, keepdims=True)
        h_scr[...] = (x * lax.rsqrt(ms + EPS) * gpre_ref[...]).astype(BF16)
        out_ref[...] = jnp.zeros(out_ref.shape, F32)

    h = h_scr[...]
    g = jnp.dot(h, wg_ref[...], preferred_element_type=F32)
    u = jnp.dot(h, wu_ref[...], preferred_element_type=F32)
    act = (g * jax.nn.sigmoid(g) * u).astype(BF16)
    out_ref[...] += jnp.dot(act, wo_ref[...], preferred_element_type=F32)

    @pl.when(j == pl.num_programs(1) - 1)
    def _():
        ff = out_ref[...]
        ms = jnp.mean(ff * ff, axis=-1, keepdims=True)
        out_ref[...] = x_ref[...] + ff * lax.rsqrt(ms + EPS) * gpost_ref[...]


def _ffn(x2d, g_pre, w_in, w_out, g_post):
    m = x2d.shape[0]
    tm, tf = FFN_TM, FFN_TF
    nf = D_FF // tf
    return pl.pallas_call(
        _ffn_body,
        grid=(m // tm, nf),
        in_specs=[
            pl.BlockSpec((tm, D_MODEL), lambda i, j: (i, 0)),
            pl.BlockSpec((1, D_MODEL), lambda i, j: (0, 0)),
            pl.BlockSpec((D_MODEL, tf), lambda i, j: (0, j)),
            pl.BlockSpec((D_MODEL, tf), lambda i, j: (0, j + nf)),
            pl.BlockSpec((tf, D_MODEL), lambda i, j: (j, 0)),
            pl.BlockSpec((1, D_MODEL), lambda i, j: (0, 0)),
        ],
        out_specs=pl.BlockSpec((tm, D_MODEL), lambda i, j: (i, 0)),
        out_shape=jax.ShapeDtypeStruct((m, D_MODEL), F32),
        scratch_shapes=[pltpu.VMEM((tm, D_MODEL), BF16)],
        compiler_params=_cparams(("parallel", "arbitrary"), 60),
        name="ffn",
    )(x2d, g_pre, w_in, w_in, w_out, g_post)


def _prep_layer(w_in, conv_w, b_gates, g_head, w_a_out, w_b_out, b_merge, w_out,
                w_ffn_in, w_ffn_out):
    w_main = jnp.concatenate([w_in[:, :OFF_G], w_in[:, OFF_M:], w_in[:, OFF_B:OFF_M]], axis=1).astype(BF16)
    w_gate = jnp.pad(w_in[:, OFF_G:OFF_B], ((0, 0), (0, LANES - N_GATES))).astype(BF16)
    conv_w8 = jnp.pad(conv_w, ((0, SUBLANES - CONV_W), (0, 0)))
    bias_row = jnp.pad(b_gates[None, :], ((0, 0), (0, LANES - N_GATES)))
    return dict(
        w_main=w_main, w_gate=w_gate, conv_w8=conv_w8, bias_row=bias_row,
        g_head=g_head[None, :], wa=w_a_out.astype(BF16), wb=w_b_out.astype(BF16),
        b_merge=b_merge[None, :], wo=w_out.astype(BF16),
        w_ffn_in=w_ffn_in.astype(BF16), w_ffn_out=w_ffn_out.astype(BF16),
    )


def _layer(x, p, g_pre_mix, g_post_mix, g_pre_ffn, g_post_ffn):
    b, t, d = x.shape
    m = b * t
    x2d = x.reshape(m, d)
    z, zu, zg = _inproj(x2d, g_pre_mix[None, :], p["w_main"], p["w_gate"])
    z3 = z.reshape(b, t, Z_W)
    qc = _conv_silu(z3, p["conv_w8"], keys=False)
    ktc = _conv_silu(z3, p["conv_w8"], keys=True)
    gcol, grow = _gate_scans(zg.reshape(b, t, LANES), p["bias_row"])
    hf, hb = _mlstm(qc, ktc, z3, gcol, grow)
    fb = _fourier(zu.reshape(b, t, W_B))
    x1 = _outproj(hf.reshape(m, W_V), hb.reshape(m, W_V), z, p["g_head"], fb.reshape(m, W_B),
                  p["b_merge"], x2d, p["wa"], p["wb"], p["wo"], g_post_mix[None, :])
    y = _ffn(x1, g_pre_ffn[None, :], p["w_ffn_in"], p["w_ffn_out"], g_post_ffn[None, :])
    return y.reshape(b, t, d)


def kernel(x_prompt, x_sample, g_pre_mix, w_in, conv_w, b_gates, g_head, w_a_out, w_b_out, b_merge,
           w_out, g_post_mix, g_pre_ffn, w_ffn_in, w_ffn_out, g_post_ffn):
    depth = w_in.shape[0]
    layers = [
        _prep_layer(w_in[l], conv_w[l], b_gates[l], g_head[l], w_a_out[l], w_b_out[l], b_merge[l],
                    w_out[l], w_ffn_in[l], w_ffn_out[l])
        for l in range(depth)
    ]

    def trunk(x):
        for l in range(depth):
            x = _layer(x, layers[l], g_pre_mix[l], g_post_mix[l], g_pre_ffn[l], g_post_ffn[l])
        return x

    return (trunk(x_prompt), trunk(x_sample))
```

```python
import functools

import numpy as np
import jax
import jax.numpy as jnp
from jax import lax
from jax.experimental import pallas as pl
from jax.experimental.pallas import tpu as pltpu

F32 = jnp.float32
BF16 = jnp.bfloat16

D_MODEL = 2048
N_HEADS = 8
DK = 128
DV = 256
W_QK = N_HEADS * DK
W_V = N_HEADS * DV
N_GATES = 4 * N_HEADS
CONV_W = 5
CHUNK = 128
N_GROUPS_B = 8
GROUP_B = 128
W_B = N_GROUPS_B * GROUP_B
D_FF = 5632
OFF_V = 2 * W_QK
OFF_O = OFF_V + W_V
OFF_G = OFF_O + W_V
OFF_B = OFF_G + N_GATES
OFF_M = OFF_B + W_B
EPS = 1e-6

LANES = 128
SUBLANES = 8
MIB = 1024 * 1024


def _cparams(sem, vmem_mib):
    return pltpu.CompilerParams(dimension_semantics=sem, vmem_limit_bytes=vmem_mib * MIB)


IN_TM = 1024
IN_TN = 1024
ZC_QK = 0
ZC_V = ZC_QK + 2 * W_QK
ZC_O = ZC_V + W_V
ZC_MA = ZC_O + W_V
ZC_MB = ZC_MA + D_MODEL
ZC_U = ZC_MB + D_MODEL
Z_W = ZC_U + W_B
IN_NJ = Z_W // IN_TN
assert IN_TN == W_B and ZC_U == (IN_NJ - 1) * IN_TN


def _inproj_body(x_ref, g_ref, w_ref, wg_ref, z_ref, zu_ref, zg_ref, h_scr):
    j = pl.program_id(1)

    @pl.when(j == 0)
    def _():
        x = x_ref[...]
        ms = jnp.mean(x * x, axis=-1, keepdims=True)
        h = (x * lax.rsqrt(ms + EPS) * g_ref[...]).astype(BF16)
        h_scr[...] = h
        zg_ref[...] = jnp.dot(h, wg_ref[...], preferred_element_type=F32)

    acc = jnp.dot(h_scr[...], w_ref[...], preferred_element_type=F32)
    z_ref[...] = acc.astype(BF16)
    zu_ref[...] = acc


def _inproj(x2d, g, w_main, w_gate):
    m = x2d.shape[0]
    tm, tn = IN_TM, IN_TN
    out_shapes = (
        jax.ShapeDtypeStruct((m, Z_W), BF16),
        jax.ShapeDtypeStruct((m, W_B), F32),
        jax.ShapeDtypeStruct((m, LANES), F32),
    )
    return pl.pallas_call(
        _inproj_body,
        grid=(m // tm, IN_NJ),
        in_specs=[
            pl.BlockSpec((tm, D_MODEL), lambda i, j: (i, 0)),
            pl.BlockSpec((1, D_MODEL), lambda i, j: (0, 0)),
            pl.BlockSpec((D_MODEL, tn), lambda i, j: (0, j)),
            pl.BlockSpec((D_MODEL, LANES), lambda i, j: (0, 0)),
        ],
        out_specs=(
            pl.BlockSpec((tm, tn), lambda i, j: (i, j)),
            pl.BlockSpec((tm, W_B), lambda i, j: (i, 0)),
            pl.BlockSpec((tm, LANES), lambda i, j: (i, 0)),
        ),
        out_shape=out_shapes,
        scratch_shapes=[pltpu.VMEM((tm, D_MODEL), BF16)],
        compiler_params=_cparams(("parallel", "arbitrary"), 56),
        name="inproj",
    )(x2d, g, w_main, w_gate)


CONV_TQ = 256
CONV_CW = W_QK
CONV_HALO = 64
CONV_NQ = W_QK // CONV_CW
CONV_WIN = 2 * CHUNK
CONV_TAPS = tuple(d for d in range(-(CONV_W // 2), CONV_W // 2 + 1) if d != 0)


@functools.lru_cache(maxsize=None)
def _shift_matrix():
    m = np.zeros((len(CONV_TAPS) * CHUNK, CONV_WIN), np.float32)
    for k, d in enumerate(CONV_TAPS):
        m[k * CHUNK + np.arange(CHUNK), CONV_HALO + np.arange(CHUNK) + d] = 1.0
    return m.astype(BF16)


def _conv_body(prev_ref, cur_ref, next_ref, w_ref, s_ref, out_ref, *, keys):
    i = pl.program_id(1)
    n = pl.num_programs(1)
    tq = cur_ref.shape[0]
    hl = CONV_HALO
    pv = jnp.where(i > 0, prev_ref[...], jnp.zeros_like(prev_ref[...]))
    nx = jnp.where(i < n - 1, next_ref[...], jnp.zeros_like(next_ref[...]))
    ext = jnp.concatenate([pv, cur_ref[...], nx], axis=0)
    shift = s_ref[...]
    parts = []
    for r in range(tq // CHUNK):
        win = ext[r * CHUNK:r * CHUNK + CONV_WIN]
        sh = jnp.dot(shift, win, preferred_element_type=F32)
        acc = win[hl:hl + CHUNK].astype(F32) * w_ref[CONV_W // 2:CONV_W // 2 + 1, :]
        for k, d in enumerate(CONV_TAPS):
            j = d + CONV_W // 2
            acc = acc + sh[k * CHUNK:(k + 1) * CHUNK] * w_ref[j:j + 1, :]
        parts.append(acc * jax.nn.sigmoid(acc))
    if keys:
        for c, y in enumerate(parts):
            out_ref[c] = (y * (DK ** -0.5)).T.astype(BF16)
    else:
        for c, y in enumerate(parts):
            out_ref[c * CHUNK:(c + 1) * CHUNK, :] = y.astype(BF16)


def _conv_silu(z3, conv_w8, keys):
    b, t, _ = z3.shape
    tq, cw, hl = CONV_TQ, CONV_CW, CONV_HALO
    r = tq // hl
    nh = t // hl
    c0 = CONV_NQ if keys else 0
    if keys:
        out_spec = pl.BlockSpec((None, tq // CHUNK, cw, CHUNK), lambda bb, i, k: (bb, i, k, 0))
        out_shape = jax.ShapeDtypeStruct((b, t // CHUNK, W_QK, CHUNK), BF16)
    else:
        out_spec = pl.BlockSpec((None, tq, cw), lambda bb, i, k: (bb, i, k))
        out_shape = jax.ShapeDtypeStruct((b, t, W_QK), BF16)
    shift = jnp.asarray(_shift_matrix())
    return pl.pallas_call(
        functools.partial(_conv_body, keys=keys),
        grid=(b, t // tq, CONV_NQ),
        in_specs=[
            pl.BlockSpec((None, hl, cw), lambda bb, i, k: (bb, jnp.maximum(i * r - 1, 0), c0 + k)),
            pl.BlockSpec((None, tq, cw), lambda bb, i, k: (bb, i, c0 + k)),
            pl.BlockSpec((None, hl, cw), lambda bb, i, k: (bb, jnp.minimum((i + 1) * r, nh - 1), c0 + k)),
            pl.BlockSpec((SUBLANES, cw), lambda bb, i, k: (0, c0 + k)),
            pl.BlockSpec(shift.shape, lambda bb, i, k: (0, 0)),
        ],
        out_specs=out_spec,
        out_shape=out_shape,
        compiler_params=_cparams(("parallel", "parallel", "parallel"), 40),
        name="conv_silu_k" if keys else "conv_silu_q",
    )(z3, z3, z3, conv_w8, shift)


GATE_TG = 1024


def _log_sigmoid(x):
    return jnp.minimum(x, 0.0) - jnp.log1p(jnp.exp(-jnp.abs(x)))


def _gates_body(zg_ref, bias_ref, col_ref, row_ref):
    g = zg_ref[...] + bias_ref[...]
    gt = g.T
    h = N_HEADS
    li_f, lf_f = gt[0:h], _log_sigmoid(gt[h:2 * h])
    li_b, lf_b = gt[2 * h:3 * h], _log_sigmoid(gt[3 * h:4 * h])
    tg = gt.shape[1]
    pos = lax.broadcasted_iota(jnp.int32, (h, tg), 1) & (CHUNK - 1)

    def scan(x, op, fill, reverse):
        s = 1
        while s < CHUNK:
            if reverse:
                sh = pltpu.roll(x, tg - s, 1)
                ok = pos < CHUNK - s
            else:
                sh = pltpu.roll(x, s, 1)
                ok = pos >= s
            x = op(x, jnp.where(ok, sh, fill))
            s *= 2
        return x

    b_f = scan(lf_f, jnp.add, 0.0, False)
    b_b = scan(lf_b, jnp.add, 0.0, True)
    w_f = li_f - b_f
    w_b = li_b - b_b
    a_f = scan(w_f, jnp.maximum, -jnp.inf, False)
    a_b = scan(w_b, jnp.maximum, -jnp.inf, True)
    row_ref[...] = jnp.concatenate([w_f, w_b], axis=0)
    pad = jnp.zeros((LANES - 4 * h, tg), F32)
    col_ref[...] = jnp.concatenate([b_f, a_f, b_b, a_b, pad], axis=0).T


def _gate_scans(zg3, bias_row):
    b, t, _ = zg3.shape
    tg = GATE_TG
    return pl.pallas_call(
        _gates_body,
        grid=(b, t // tg),
        in_specs=[
            pl.BlockSpec((None, tg, LANES), lambda bb, i: (bb, i, 0)),
            pl.BlockSpec((1, LANES), lambda bb, i: (0, 0)),
        ],
        out_specs=(
            pl.BlockSpec((None, tg, LANES), lambda bb, i: (bb, i, 0)),
            pl.BlockSpec((None, 2 * N_HEADS, tg), lambda bb, i: (bb, 0, i)),
        ),
        out_shape=(
            jax.ShapeDtypeStruct((b, t, LANES), F32),
            jax.ShapeDtypeStruct((b, 2 * N_HEADS, t), F32),
        ),
        compiler_params=_cparams(("parallel", "parallel"), 32),
        name="gate_scans",
    )(zg3, bias_row)


ST_W = DV + LANES
MLSTM_NCH = 2 * N_HEADS


def _chain_refs(c, refs):
    d, hd = divmod(c, N_HEADS)
    return d, hd, d == 1, refs[d]


def _mlstm_gates(c, refs, m_ref, p_scr, lhs_scr, etb_scr, kwt_scr, wc_scr):
    L = CHUNK
    d, hd, reverse, (q_ref, kt_ref, v_ref, col_ref, row_ref, h_ref) = _chain_refs(c, refs)
    last = 0 if reverse else L - 1
    cb = 2 * N_HEADS * d + hd
    colblk = col_ref[...]
    lane_b = jnp.full((L, LANES), cb, jnp.int32)
    b_b = jnp.take_along_axis(colblk, lane_b, axis=1)
    a_b = jnp.take_along_axis(colblk, lane_b + N_HEADS, axis=1)
    wrow = row_ref[N_HEADS * d + hd:N_HEADS * d + hd + 1, :]
    m11 = m_ref[d, hd:hd + 1, 0:1]
    mxb = jnp.maximum(a_b, m11)
    mx_last = mxb[last:last + 1, :]
    b_last = b_b[last:last + 1, :]
    p_scr[c] = jnp.exp(wrow - mxb)
    q = q_ref[:, hd * DK:(hd + 1) * DK]
    lhs_scr[c, :, L:] = (q.astype(F32) * jnp.exp(m11 - mxb)).astype(BF16)
    etb_scr[c] = jnp.exp(-(b_b + mxb))
    kt = kt_ref[0, hd * DK:(hd + 1) * DK, :]
    kwt_scr[c] = (kt.astype(F32) * jnp.exp(wrow - mx_last)).astype(BF16)
    wc_scr[c] = jnp.broadcast_to(jnp.exp(m11 - mx_last), (SUBLANES, LANES))
    m_ref[d, hd:hd + 1, :] = b_last + mx_last


def _mlstm_scores(c, refs, p_scr, lhs_scr):
    L = CHUNK
    d, hd, reverse, (q_ref, kt_ref, v_ref, col_ref, row_ref, h_ref) = _chain_refs(c, refs)
    t_idx = lax.broadcasted_iota(jnp.int32, (L, L), 0)
    s_idx = lax.broadcasted_iota(jnp.int32, (L, L), 1)
    mask = (s_idx >= t_idx) if reverse else (s_idx <= t_idx)
    q = q_ref[:, hd * DK:(hd + 1) * DK]
    kt = kt_ref[0, hd * DK:(hd + 1) * DK, :]
    s_qk = jnp.dot(q, kt, preferred_element_type=F32)
    lhs_scr[c, :, :L] = jnp.where(mask, s_qk * p_scr[c], 0.0).astype(BF16)


def _mlstm_readout(c, refs, st_ref, lhs_scr, etb_scr, kwt_scr, wc_scr):
    L = CHUNK
    d, hd, reverse, (q_ref, kt_ref, v_ref, col_ref, row_ref, h_ref) = _chain_refs(c, refs)
    v = v_ref[:, hd * DV:(hd + 1) * DV]
    st = st_ref[d, hd]
    vext = jnp.concatenate([v, jnp.ones((L, LANES), BF16)], axis=1)
    rhs = jnp.concatenate([vext, st.astype(BF16)], axis=0)
    res = jnp.dot(lhs_scr[c], rhs, preferred_element_type=F32)
    den = jnp.maximum(jnp.abs(res[:, DV:]), etb_scr[c])
    inv = 1.0 / den
    hout = res[:, :DV] * jnp.concatenate([inv, inv], axis=1)
    h_ref[:, hd * DV:(hd + 1) * DV] = hout.astype(h_ref.dtype)
    upd = jnp.dot(kwt_scr[c], vext, preferred_element_type=F32)
    st_ref[d, hd] = wc_scr[c, 0:1, 0:1] * st + upd


def _mlstm_body(qf_ref, qb_ref, ktf_ref, ktb_ref, vf_ref, vb_ref, colf_ref, colb_ref, rowf_ref, rowb_ref,
                hf_ref, hb_ref, st_ref, m_ref, p_scr, lhs_scr, etb_scr, kwt_scr, wc_scr):
    step = pl.program_id(1)

    @pl.when(step == 0)
    def _():
        st_ref[...] = jnp.zeros(st_ref.shape, F32)
        m_ref[...] = jnp.zeros(m_ref.shape, F32)

    refs = ((qf_ref, ktf_ref, vf_ref, colf_ref, rowf_ref, hf_ref),
            (qb_ref, ktb_ref, vb_ref, colb_ref, rowb_ref, hb_ref))

    @pl.when(step >= 0)
    def _():
        for c in range(MLSTM_NCH):
            _mlstm_gates(c, refs, m_ref, p_scr, lhs_scr, etb_scr, kwt_scr, wc_scr)

    @pl.when(step >= -1)
    def _():
        for c in range(MLSTM_NCH):
            _mlstm_scores(c, refs, p_scr, lhs_scr)

    @pl.when(step >= -2)
    def _():
        for c in range(MLSTM_NCH):
            _mlstm_readout(c, refs, st_ref, lhs_scr, etb_scr, kwt_scr, wc_scr)


def _mlstm(qc, ktc, z3, gcol, grow):
    b, t, _ = qc.shape
    L = CHUNK
    ns = t // L
    vcol = ZC_V // W_V
    nch = MLSTM_NCH
    fwd3 = lambda bb, i: (bb, i, 0)
    bwd3 = lambda bb, i: (bb, ns - 1 - i, 0)
    return pl.pallas_call(
        _mlstm_body,
        grid=(b, ns),
        in_specs=[
            pl.BlockSpec((None, L, W_QK), fwd3),
            pl.BlockSpec((None, L, W_QK), bwd3),
            pl.BlockSpec((None, 1, W_QK, CHUNK), lambda bb, i: (bb, i, 0, 0)),
            pl.BlockSpec((None, 1, W_QK, CHUNK), lambda bb, i: (bb, ns - 1 - i, 0, 0)),
            pl.BlockSpec((None, L, W_V), lambda bb, i: (bb, i, vcol)),
            pl.BlockSpec((None, L, W_V), lambda bb, i: (bb, ns - 1 - i, vcol)),
            pl.BlockSpec((None, L, LANES), fwd3),
            pl.BlockSpec((None, L, LANES), bwd3),
            pl.BlockSpec((None, 2 * N_HEADS, L), lambda bb, i: (bb, 0, i)),
            pl.BlockSpec((None, 2 * N_HEADS, L), lambda bb, i: (bb, 0, ns - 1 - i)),
        ],
        out_specs=(
            pl.BlockSpec((None, L, W_V), fwd3),
            pl.BlockSpec((None, L, W_V), bwd3),
        ),
        out_shape=(
            jax.ShapeDtypeStruct((b, t, W_V), BF16),
            jax.ShapeDtypeStruct((b, t, W_V), BF16),
        ),
        scratch_shapes=[
            pltpu.VMEM((2, N_HEADS, DK, ST_W), F32),
            pltpu.VMEM((2, N_HEADS, LANES), F32),
            pltpu.VMEM((nch, L, L), F32),
            pltpu.VMEM((nch, L, 2 * L), BF16),
            pltpu.VMEM((nch, L, LANES), F32),
            pltpu.VMEM((nch, DK, L), BF16),
            pltpu.VMEM((nch, SUBLANES, LANES), F32),
        ],
        compiler_params=_cparams(("parallel", "arbitrary"), 40),
        name="mlstm",
    )(qc, qc, ktc, ktc, z3, z3, gcol, gcol, grow, grow)


FFT_N2 = 128
FFT_R = SUBLANES
FFT1_CB = W_B
FFT2_CB = 512


@functools.lru_cache(maxsize=None)
def _fft_tables(t):
    n2 = FFT_N2
    n1 = t // n2
    k1 = np.arange(n1)[:, None].astype(np.float64)
    t1 = np.arange(n1)[None, :].astype(np.float64)
    t2 = np.arange(n2)[:, None, None].astype(np.float64)
    ang = 2.0 * np.pi * (k1 * t1 / n1)[None] + 2.0 * np.pi * (k1[None] * t2 / t)
    tab1 = np.concatenate([np.cos(ang), np.sin(ang)], axis=1) / np.sqrt(n1)
    c = np.arange(GROUP_B)[:, None].astype(np.float64)
    cc = np.arange(GROUP_B)[None, :].astype(np.float64)
    angc = 2.0 * np.pi * c * cc / GROUP_B
    cg, sg = np.cos(angc) / np.sqrt(GROUP_B), np.sin(angc) / np.sqrt(GROUP_B)
    chan = np.block([[cg, sg], [-sg, cg]])
    k2 = np.arange(n2)[:, None].astype(np.float64)
    tt = np.arange(n2)[None, :].astype(np.float64)
    ang2 = 2.0 * np.pi * k2 * tt / n2
    tab2 = np.concatenate([np.cos(ang2), -np.sin(ang2)], axis=1) / np.sqrt(n2)
    return tuple(np.asarray(a.astype(np.float32)).astype(BF16) for a in (tab1, chan, tab2))


def _pack_complex(re, im):
    hi = lax.bitcast_convert_type(re.astype(BF16).astype(F32), jnp.uint32)
    lo = lax.bitcast_convert_type(im.astype(BF16).astype(F32), jnp.uint32)
    return hi | (lo >> 16)


def _unpack_complex(w):
    re = lax.bitcast_convert_type(w & jnp.uint32(0xFFFF0000), F32)
    im = lax.bitcast_convert_type(w << 16, F32)
    return re.astype(BF16), im.astype(BF16)


def _fft1_body(u_ref, tab_ref, e_ref, a_ref, sin_scr, uri_scr, sa_scr):
    n1 = u_ref.shape[0]
    ng = u_ref.shape[2] // LANES
    for g in range(ng):
        sin_scr[g] = u_ref[:, :, g * LANES:(g + 1) * LANES].reshape(n1 * FFT_R, LANES)
    for r in range(FFT_R):
        x = jnp.concatenate(
            [sin_scr.at[g][pl.ds(r, n1, stride=FFT_R), :] for g in range(ng)], axis=1)
        uri = jnp.dot(tab_ref[r], x.astype(BF16), preferred_element_type=F32)
        uri_scr[r] = uri.astype(BF16)
    e = e_ref[...]
    for g in range(ng):
        lanes = slice(g * LANES, (g + 1) * LANES)
        lhs = jnp.concatenate([uri_scr[:, :n1, lanes], uri_scr[:, n1:, lanes]], axis=-1)
        rr = jnp.dot(lhs.reshape(FFT_R * n1, 2 * LANES), e, preferred_element_type=F32)
        rr = rr.reshape(FFT_R, n1, 2 * LANES)
        for r in range(FFT_R):
            sa_scr.at[g][pl.ds(r, n1, stride=FFT_R), :] = _pack_complex(rr[r, :, :LANES], rr[r, :, LANES:])
    for g in range(ng):
        a_ref[:, :, g * LANES:(g + 1) * LANES] = sa_scr[g].reshape(n1, FFT_R, LANES)


def _fft2_body(a_ref, tab_ref, y_ref, s_scr):
    n2 = a_ref.shape[1]
    ng = a_ref.shape[2] // LANES
    tab = tab_ref[...]
    for r in range(FFT_R):
        br, bi = _unpack_complex(a_ref[r])
        bri = jnp.concatenate([br, bi], axis=0)
        y = jnp.dot(tab, bri, preferred_element_type=F32)
        for g in range(ng):
            s_scr.at[g][pl.ds(r, n2, stride=FFT_R), :] = y[:, g * LANES:(g + 1) * LANES]
    for g in range(ng):
        y_ref[:, :, g * LANES:(g + 1) * LANES] = s_scr[g].reshape(n2, FFT_R, LANES)


def _fourier(zu3):
    b, t, c = zu3.shape
    n2 = FFT_N2
    n1 = t // n2
    tab1, chan, tab2 = _fft_tables(t)
    tab1, chan, tab2 = jnp.asarray(tab1), jnp.asarray(chan), jnp.asarray(tab2)
    u4 = zu3.reshape(b, n1, n2, c)
    cb1, cb2, r = FFT1_CB, FFT2_CB, FFT_R
    a = pl.pallas_call(
        _fft1_body,
        grid=(b, n2 // r, c // cb1),
        in_specs=[
            pl.BlockSpec((None, n1, r, cb1), lambda bb, j, k: (bb, 0, j, k)),
            pl.BlockSpec((r, 2 * n1, n1), lambda bb, j, k: (j, 0, 0)),
            pl.BlockSpec((2 * GROUP_B, 2 * GROUP_B), lambda bb, j, k: (0, 0)),
        ],
        out_specs=pl.BlockSpec((None, n1, r, cb1), lambda bb, j, k: (bb, 0, j, k)),
        out_shape=jax.ShapeDtypeStruct((b, n1, n2, c), jnp.uint32),
        scratch_shapes=[
            pltpu.VMEM((cb1 // LANES, n1 * r, LANES), F32),
            pltpu.VMEM((r, 2 * n1, cb1), BF16),
            pltpu.VMEM((cb1 // LANES, n1 * r, LANES), jnp.uint32),
        ],
        compiler_params=_cparams(("parallel", "parallel", "parallel"), 48),
        name="fft_stage1",
    )(u4, tab1, chan)
    y = pl.pallas_call(
        _fft2_body,
        grid=(b, n1 // r, c // cb2),
        in_specs=[
            pl.BlockSpec((None, r, n2, cb2), lambda bb, j, k: (bb, j, 0, k)),
            pl.BlockSpec((n2, 2 * n2), lambda bb, j, k: (0, 0)),
        ],
        out_specs=pl.BlockSpec((None, n2, r, cb2), lambda bb, j, k: (bb, 0, j, k)),
        out_shape=jax.ShapeDtypeStruct((b, n2, n1, c), F32),
        scratch_shapes=[pltpu.VMEM((cb2 // LANES, n2 * r, LANES), F32)],
        compiler_params=_cparams(("parallel", "parallel", "parallel"), 40),
        name="fft_stage2",
    )(a, tab2)
    return y.reshape(b, t, c)


OUT_TM = 256


def _rowsum_lanes(x, ones_bf16):
    return jnp.dot(x.astype(BF16), ones_bf16, preferred_element_type=F32)


def _outproj_body(hf_ref, hb_ref, o_ref, gh_ref, fb_ref, ma_ref, mb_ref, bma_ref, bmb_ref, x_ref,
                  wa_ref, wb_ref, wo_ref, gpost_ref, out_ref):
    ones = jnp.ones((DV, LANES), BF16)
    ha = hf_ref[...].astype(F32) + hb_ref[...].astype(F32)
    parts = []
    for hd in range(N_HEADS):
        hh = ha[:, hd * DV:(hd + 1) * DV]
        ssum = _rowsum_lanes(hh * hh, ones)
        inv = lax.rsqrt(ssum * (1.0 / DV) + EPS)
        parts.append(hh * jnp.concatenate([inv, inv], axis=1))
    han = jnp.concatenate(parts, axis=1) * gh_ref[...]
    han = han * jax.nn.sigmoid(o_ref[...].astype(F32))
    ya = jnp.dot(han.astype(BF16), wa_ref[...], preferred_element_type=F32)
    yb = jnp.dot(fb_ref[...].astype(BF16), wb_ref[...], preferred_element_type=F32)
    ga = jax.nn.sigmoid(ma_ref[...].astype(F32) + bma_ref[...])
    gb = jax.nn.sigmoid(mb_ref[...].astype(F32) + bmb_ref[...])
    mixin = (ga * ya + gb * yb).astype(BF16)
    mix = jnp.dot(mixin, wo_ref[...], preferred_element_type=F32)
    ms = jnp.mean(mix * mix, axis=-1, keepdims=True)
    out_ref[...] = x_ref[...] + mix * lax.rsqrt(ms + EPS) * gpost_ref[...]


def _outproj(hf, hb, z, g_head, fb, b_merge, x2d, wa, wb, wo, g_post):
    m = x2d.shape[0]
    tm = OUT_TM
    row = lambda i: (i, 0)
    const = lambda i: (0, 0)
    resident = functools.partial(pl.BlockSpec, index_map=const, pipeline_mode=pl.Buffered(1))
    return pl.pallas_call(
        _outproj_body,
        grid=(m // tm,),
        in_specs=[
            pl.BlockSpec((tm, W_V), row),
            pl.BlockSpec((tm, W_V), row),
            pl.BlockSpec((tm, W_V), lambda i: (i, ZC_O // W_V)),
            pl.BlockSpec((1, W_V), const),
            pl.BlockSpec((tm, W_B), row),
            pl.BlockSpec((tm, D_MODEL), lambda i: (i, ZC_MA // D_MODEL)),
            pl.BlockSpec((tm, D_MODEL), lambda i: (i, ZC_MB // D_MODEL)),
            pl.BlockSpec((1, D_MODEL), lambda i: (0, 0)),
            pl.BlockSpec((1, D_MODEL), lambda i: (0, 1)),
            pl.BlockSpec((tm, D_MODEL), row),
            resident((W_V, D_MODEL)),
            resident((W_B, D_MODEL)),
            resident((D_MODEL, D_MODEL)),
            pl.BlockSpec((1, D_MODEL), const),
        ],
        out_specs=pl.BlockSpec((tm, D_MODEL), row),
        out_shape=jax.ShapeDtypeStruct((m, D_MODEL), F32),
        compiler_params=_cparams(("parallel",), 56),
        name="outproj",
    )(hf, hb, z, g_head, fb, z, z, b_merge, b_merge, x2d, wa, wb, wo, g_post)


FFN_TM = 1024
FFN_TF = 256


def _ffn_body(x_ref, gpre_ref, wg_ref, wu_ref, wo_ref, gpost_ref, out_ref, h_scr):
    j = pl.program_id(1)

    @pl.when(j == 0)
    def _():
        x = x_ref[...]
        ms = jnp.mean(x * x, axis=-1, keepdims=True)
        h_scr[...] = (x * lax.rsqrt(ms + EPS) * gpre_ref[...]).astype(BF16)
        out_ref[...] = jnp.zeros(out_ref.shape, F32)

    h = h_scr[...]
    g = jnp.dot(h, wg_ref[...], preferred_element_type=F32)
    u = jnp.dot(h, wu_ref[...], preferred_element_type=F32)
    act = (g * jax.nn.sigmoid(g) * u).astype(BF16)
    out_ref[...] += jnp.dot(act, wo_ref[...], preferred_element_type=F32)

    @pl.when(j == pl.num_programs(1) - 1)
    def _():
        ff = out_ref[...]
        ms = jnp.mean(ff * ff, axis=-1, keepdims=True)
        out_ref[...] = x_ref[...] + ff * lax.rsqrt(ms + EPS) * gpost_ref[...]


def _ffn(x2d, g_pre, w_in, w_out, g_post):
    m = x2d.shape[0]
    tm, tf = FFN_TM, FFN_TF
    nf = D_FF // tf
    return pl.pallas_call(
        _ffn_body,
        grid=(m // tm, nf),
        in_specs=[
            pl.BlockSpec((tm, D_MODEL), lambda i, j: (i, 0)),
            pl.BlockSpec((1, D_MODEL), lambda i, j: (0, 0)),
            pl.BlockSpec((D_MODEL, tf), lambda i, j: (0, j)),
            pl.BlockSpec((D_MODEL, tf), lambda i, j: (0, j + nf)),
            pl.BlockSpec((tf, D_MODEL), lambda i, j: (j, 0)),
            pl.BlockSpec((1, D_MODEL), lambda i, j: (0, 0)),
        ],
        out_specs=pl.BlockSpec((tm, D_MODEL), lambda i, j: (i, 0)),
        out_shape=jax.ShapeDtypeStruct((m, D_MODEL), F32),
        scratch_shapes=[pltpu.VMEM((tm, D_MODEL), BF16)],
        compiler_params=_cparams(("parallel", "arbitrary"), 60),
        name="ffn",
    )(x2d, g_pre, w_in, w_in, w_out, g_post)


def _prep_layer(w_in, conv_w, b_gates, g_head, w_a_out, w_b_out, b_merge, w_out,
                w_ffn_in, w_ffn_out):
    w_main = jnp.concatenate([w_in[:, :OFF_G], w_in[:, OFF_M:], w_in[:, OFF_B:OFF_M]], axis=1).astype(BF16)
    w_gate = jnp.pad(w_in[:, OFF_G:OFF_B], ((0, 0), (0, LANES - N_GATES))).astype(BF16)
    conv_w8 = jnp.pad(conv_w, ((0, SUBLANES - CONV_W), (0, 0)))
    bias_row = jnp.pad(b_gates[None, :], ((0, 0), (0, LANES - N_GATES)))
    return dict(
        w_main=w_main, w_gate=w_gate, conv_w8=conv_w8, bias_row=bias_row,
        g_head=g_head[None, :], wa=w_a_out.astype(BF16), wb=w_b_out.astype(BF16),
        b_merge=b_merge[None, :], wo=w_out.astype(BF16),
        w_ffn_in=w_ffn_in.astype(BF16), w_ffn_out=w_ffn_out.astype(BF16),
    )


def _layer(x, p, g_pre_mix, g_post_mix, g_pre_ffn, g_post_ffn):
    b, t, d = x.shape
    m = b * t
    x2d = x.reshape(m, d)
    z, zu, zg = _inproj(x2d, g_pre_mix[None, :], p["w_main"], p["w_gate"])
    z3 = z.reshape(b, t, Z_W)
    qc = _conv_silu(z3, p["conv_w8"], keys=False)
    ktc = _conv_silu(z3, p["conv_w8"], keys=True)
    gcol, grow = _gate_scans(zg.reshape(b, t, LANES), p["bias_row"])
    hf, hb = _mlstm(qc, ktc, z3, gcol, grow)
    fb = _fourier(zu.reshape(b, t, W_B))
    x1 = _outproj(hf.reshape(m, W_V), hb.reshape(m, W_V), z, p["g_head"], fb.reshape(m, W_B),
                  p["b_merge"], x2d, p["wa"], p["wb"], p["wo"], g_post_mix[None, :])
    y = _ffn(x1, g_pre_ffn[None, :], p["w_ffn_in"], p["w_ffn_out"], g_post_ffn[None, :])
    return y.reshape(b, t, d)


def kernel(x_prompt, x_sample, g_pre_mix, w_in, conv_w, b_gates, g_head, w_a_out, w_b_out, b_merge,
           w_out, g_post_mix, g_pre_ffn, w_ffn_in, w_ffn_out, g_post_ffn):
    depth = w_in.shape[0]
    layers = [
        _prep_layer(w_in[l], conv_w[l], b_gates[l], g_head[l], w_a_out[l], w_b_out[l], b_merge[l],
                    w_out[l], w_ffn_in[l], w_ffn_out[l])
        for l in range(depth)
    ]

    def trunk(x):
        for l in range(depth):
            x = _layer(x, layers[l], g_pre_mix[l], g_post_mix[l], g_pre_ffn[l], g_post_ffn[l])
        return x

    return (trunk(x_prompt), trunk(x_sample))
```
